```python
import jax, jax.numpy as jnp
from jax import lax
import numpy as np

D_MODEL = 1024
BATCH = 2
SEQ = 8192
DEPTH = 2

PLE_DIM = 256
HEAD_DIM = 64
N_Q_HEADS = 16
N_KV_HEADS = 4
Q_PER_KV = N_Q_HEADS // N_KV_HEADS
WINDOW = 128
BLOCK = 128
ROPE_THETA = 10000.0
ATT_Q = N_Q_HEADS * HEAD_DIM
ATT_KV = N_KV_HEADS * HEAD_DIM
RW_HEADS = 16
RW_HEAD = 64
RW_DIM = RW_HEADS * RW_HEAD
DECAY_RANK = 64
ICLR_RANK = 64
GATE_RANK = 160
RW_SHIFT_COLS = 3 * RW_DIM + DECAY_RANK + ICLR_RANK + GATE_RANK
D_IN = ATT_Q + 2 * ATT_KV + RW_SHIFT_COLS + 2 * D_MODEL
FFN_DIM = 2816
CONV_W = 3
NORM_EPS = 1e-6
GN_EPS = 64e-5

kernel_name = 'hybrid_swa_sink_rwkv7_convffn_ple'


def _split(z, sizes):
    out, off = [], 0
    for s in sizes:
        out.append(z[..., off:off + s])
        off += s
    return out


def rmsnorm(x, g):
    xf = x.astype(jnp.float32)
    y = xf * lax.rsqrt(jnp.mean(xf * xf, axis=-1, keepdims=True) + NORM_EPS)
    return (y * g.astype(jnp.float32)).astype(x.dtype)


def rope(t, positions):
    half = HEAD_DIM // 2
    inv_freq = jnp.power(ROPE_THETA, -jnp.arange(half, dtype=jnp.float32) / half)
    ang = positions.astype(jnp.float32)[..., None] * inv_freq
    cos = jnp.cos(ang)[:, :, None, :]
    sin = jnp.sin(ang)[:, :, None, :]
    tf = t.astype(jnp.float32)
    t1, t2 = tf[..., :half], tf[..., half:]
    return jnp.concatenate([t1 * cos - t2 * sin, t2 * cos + t1 * sin], axis=-1).astype(t.dtype)


def sliding_window_attention(q, k, v, sinks):
    B, S = q.shape[0], q.shape[1]
    nb = S // BLOCK
    qb = q.astype(jnp.float32).reshape(B, nb, BLOCK, N_KV_HEADS, Q_PER_KV, HEAD_DIM)

    def kv_blocks(t):
        t = t.astype(jnp.float32)
        prev = jnp.pad(t, ((0, 0), (BLOCK, 0), (0, 0), (0, 0)))[:, :S]
        return jnp.concatenate([prev.reshape(B, nb, BLOCK, N_KV_HEADS, HEAD_DIM),
                                t.reshape(B, nb, BLOCK, N_KV_HEADS, HEAD_DIM)], axis=2)

    kb, vb = kv_blocks(k), kv_blocks(v)
    s = jnp.einsum('bnqhgd,bnkhd->bnhgqk', qb, kb) * (HEAD_DIM ** -0.5)
    qi = jnp.arange(BLOCK)[:, None]
    kj = jnp.arange(2 * BLOCK)[None, :]
    dist = qi + BLOCK - kj
    band = (dist >= 0) & (dist < WINDOW)
    kpos = jnp.arange(nb)[:, None, None] * BLOCK - BLOCK + kj[None]
    mask = band[None] & (kpos >= 0)
    s = jnp.where(mask[None, :, None, None], s, -jnp.inf)
    sink = sinks.astype(jnp.float32).reshape(N_KV_HEADS, Q_PER_KV)[None, None, :, :, None, None]
    m = jnp.maximum(jnp.max(s, axis=-1, keepdims=True), sink)
    pr = jnp.exp(s - m)
    pr = pr / (jnp.sum(pr, axis=-1, keepdims=True) + jnp.exp(sink - m))
    o = jnp.einsum('bnhgqk,bnkhd->bnqhgd', pr, vb)
    return o.reshape(B, S, ATT_Q).astype(q.dtype)


def rwkv7_scan(r, w, k, v, a, b):
    B, S, H, N = r.shape

    def step(state, inp):
        r_t, w_t, k_t, v_t, a_t, b_t = inp
        sa = jnp.einsum('bhvk,bhk->bhv', state, a_t)
        state = (state * w_t[:, :, None, :] + sa[..., None] * b_t[:, :, None, :]
                 + v_t[..., None] * k_t[:, :, None, :])
        return state, jnp.einsum('bhvk,bhk->bhv', state, r_t)

    xs = tuple(jnp.moveaxis(t.astype(jnp.float32), 1, 0) for t in (r, w, k, v, a, b))
    s0 = jnp.zeros((B, H, N, N), jnp.float32)
    _, y = lax.scan(step, s0, xs)
    return jnp.moveaxis(y, 0, 1)


def rwkv7_time_mix(z, mu, w0, w2, a0, a2, g2, k_k, k_a, r_k, gn_w, gn_b):
    B, S = z.shape[0], z.shape[1]
    zs = z + (jnp.pad(z, ((0, 0), (1, 0), (0, 0)))[:, :S] - z) * mu
    r, k, v, wd, ad, gd = _split(zs, (RW_DIM, RW_DIM, RW_DIM, DECAY_RANK, ICLR_RANK, GATE_RANK))
    w = -jax.nn.softplus(-(w0 + jnp.tanh(wd) @ w2)) - 0.5
    decay = jnp.exp(-jnp.exp(w.astype(jnp.float32)))
    iclr = jax.nn.sigmoid(a0 + ad @ a2)
    g = jax.nn.sigmoid(gd) @ g2

    def heads(t):
        return t.astype(jnp.float32).reshape(B, S, RW_HEADS, RW_HEAD)

    kk = heads(k * k_k)
    kk = kk / jnp.maximum(jnp.sqrt(jnp.sum(kk * kk, axis=-1, keepdims=True)), 1e-12)
    k = k * (1.0 + (iclr - 1.0) * k_a)
    rh, kh, vh, ah = heads(r), heads(k), heads(v), heads(iclr)
    y = rwkv7_scan(rh, heads(decay), kh, vh, -kk, kk * ah)
    mean = jnp.mean(y, axis=-1, keepdims=True)
    var = jnp.mean(jnp.square(y - mean), axis=-1, keepdims=True)
    y = ((y - mean) * lax.rsqrt(var + GN_EPS)).reshape(B, S, RW_DIM) * gn_w + gn_b
    bonus = jnp.sum(rh * kh * r_k.astype(jnp.float32), axis=-1, keepdims=True) * vh
    out = (y + bonus.reshape(B, S, RW_DIM)) * g
    return out.astype(z.dtype)


def causal_dwconv(u, w, b):
    C = u.shape[-1]
    y = lax.conv_general_dilated(u, w[:, None, :].astype(u.dtype), window_strides=(1,),
                                 padding=[(CONV_W - 1, 0)],
                                 dimension_numbers=('NWC', 'WIO', 'NWC'),
                                 feature_group_count=C)
    return y + b


def hybrid_layer(x, p_i, positions, norm_mix_pre, w_in, att_sinks, w_o_att, rw_mu, rw_w0,
                 rw_w2, rw_a0, rw_a2, rw_g2, rw_k_k, rw_k_a, rw_r_k, rw_gn_w, rw_gn_b,
                 w_o_rw, w_out, norm_mix_post, norm_ffn_pre, w_up, conv_w, conv_b, w_down,
                 norm_ffn_post, w_ple, w_ple_gate, norm_ple):
    B, S = x.shape[0], x.shape[1]
    h = rmsnorm(x, norm_mix_pre)
    proj = h @ w_in
    q, k, v, rw, gate_a, gate_b = _split(proj, (ATT_Q, ATT_KV, ATT_KV, RW_SHIFT_COLS, D_MODEL, D_MODEL))
    q = rope(q.reshape(B, S, N_Q_HEADS, HEAD_DIM), positions)
    k = rope(k.reshape(B, S, N_KV_HEADS, HEAD_DIM), positions)
    v = v.reshape(B, S, N_KV_HEADS, HEAD_DIM)
    y_att = sliding_window_attention(q, k, v, att_sinks) @ w_o_att
    y_rw = rwkv7_time_mix(rw, rw_mu, rw_w0, rw_w2, rw_a0, rw_a2, rw_g2, rw_k_k, rw_k_a,
                          rw_r_k, rw_gn_w, rw_gn_b) @ w_o_rw
    mixed = jax.nn.sigmoid(gate_a) * y_att + jax.nn.sigmoid(gate_b) * y_rw
    x = x + rmsnorm(mixed @ w_out, norm_mix_post)
    h = rmsnorm(x, norm_ffn_pre)
    u = causal_dwconv(h @ w_up, conv_w, conv_b)
    ua, ub = _split(u, (FFN_DIM, FFN_DIM))
    x = x + rmsnorm((jax.nn.gelu(ua) * ub) @ w_down, norm_ffn_post)
    e = p_i @ w_ple
    gate = jax.nn.sigmoid(x @ w_ple_gate)
    x = x + rmsnorm(gate * e, norm_ple)
    return x


def setup_inputs(seed: int = 0) -> dict:
    key = jax.random.key(seed)
    ks = iter(jax.random.split(key, 40))

    def nrm(shape, scale):
        return jax.random.normal(next(ks), shape, jnp.float32) * scale

    def gain(n):
        return 1.0 + nrm((DEPTH, n), 0.05)

    L = DEPTH
    return {
        'x': nrm((BATCH, SEQ, D_MODEL), 1.0),
        'p': nrm((DEPTH, BATCH, SEQ, PLE_DIM), 1.0),
        'positions': jnp.broadcast_to(jnp.arange(SEQ, dtype=jnp.int32)[None], (BATCH, SEQ)),
        'norm_mix_pre': gain(D_MODEL),
        'w_in': nrm((L, D_MODEL, D_IN), D_MODEL ** -0.5),
        'att_sinks': nrm((L, N_Q_HEADS), 0.5),
        'w_o_att': nrm((L, ATT_Q, D_MODEL), ATT_Q ** -0.5),
        'rw_mu': jax.random.uniform(next(ks), (L, RW_SHIFT_COLS), jnp.float32),
        'rw_w0': nrm((L, RW_DIM), 0.5),
        'rw_w2': nrm((L, DECAY_RANK, RW_DIM), DECAY_RANK ** -0.5),
        'rw_a0': nrm((L, RW_DIM), 0.5),
        'rw_a2': nrm((L, ICLR_RANK, RW_DIM), ICLR_RANK ** -0.5),
        'rw_g2': nrm((L, GATE_RANK, RW_DIM), GATE_RANK ** -0.5),
        'rw_k_k': 0.85 + nrm((L, RW_DIM), 0.1),
        'rw_k_a': 1.0 + nrm((L, RW_DIM), 0.1),
        'rw_r_k': nrm((L, RW_HEADS, RW_HEAD), 0.1),
        'rw_gn_w': gain(RW_DIM),
        'rw_gn_b': nrm((L, RW_DIM), 0.01),
        'w_o_rw': nrm((L, RW_DIM, D_MODEL), RW_DIM ** -0.5),
        'w_out': nrm((L, D_MODEL, D_MODEL), D_MODEL ** -0.5),
        'norm_mix_post': gain(D_MODEL),
        'norm_ffn_pre': gain(D_MODEL),
        'w_up': nrm((L, D_MODEL, 2 * FFN_DIM), D_MODEL ** -0.5),
        'conv_w': nrm((L, CONV_W, 2 * FFN_DIM), CONV_W ** -0.5),
        'conv_b': nrm((L, 2 * FFN_DIM), 0.01),
        'w_down': nrm((L, FFN_DIM, D_MODEL), FFN_DIM ** -0.5),
        'norm_ffn_post': gain(D_MODEL),
        'w_ple': nrm((L, PLE_DIM, D_MODEL), PLE_DIM ** -0.5),
        'w_ple_gate': nrm((L, D_MODEL, D_MODEL), D_MODEL ** -0.5),
        'norm_ple': gain(D_MODEL),
    }


def reference(x, p, positions, norm_mix_pre, w_in, att_sinks, w_o_att, rw_mu, rw_w0, rw_w2,
              rw_a0, rw_a2, rw_g2, rw_k_k, rw_k_a, rw_r_k, rw_gn_w, rw_gn_b, w_o_rw, w_out,
              norm_mix_post, norm_ffn_pre, w_up, conv_w, conv_b, w_down, norm_ffn_post,
              w_ple, w_ple_gate, norm_ple):
    for i in range(DEPTH):
        x = hybrid_layer(x, p[i], positions, norm_mix_pre[i], w_in[i], att_sinks[i], w_o_att[i],
                         rw_mu[i], rw_w0[i], rw_w2[i], rw_a0[i], rw_a2[i], rw_g2[i], rw_k_k[i],
                         rw_k_a[i], rw_r_k[i], rw_gn_w[i], rw_gn_b[i], w_o_rw[i], w_out[i],
                         norm_mix_post[i], norm_ffn_pre[i], w_up[i], conv_w[i], conv_b[i],
                         w_down[i], norm_ffn_post[i], w_ple[i], w_ple_gate[i], norm_ple[i])
    return x
```

```python
import functools

import jax
import jax.numpy as jnp
from jax import lax
from jax.experimental import pallas as pl
from jax.experimental.pallas import tpu as pltpu

F32 = jnp.float32
BF16 = jnp.bfloat16

D_MODEL = 1024
PLE_DIM = 256
HEAD_DIM = 64
N_Q_HEADS = 16
N_KV_HEADS = 4
Q_PER_KV = N_Q_HEADS // N_KV_HEADS
WINDOW = 128
BLOCK = 128
ROPE_THETA = 10000.0
ATT_Q = N_Q_HEADS * HEAD_DIM
ATT_KV = N_KV_HEADS * HEAD_DIM
RW_HEADS = 16
RW_HEAD = 64
RW_DIM = RW_HEADS * RW_HEAD
DECAY_RANK = 64
ICLR_RANK = 64
GATE_RANK = 160
LORA_DIM = DECAY_RANK + ICLR_RANK + GATE_RANK
LORA_PAD = 384
RW_SHIFT_COLS = 3 * RW_DIM + LORA_DIM
FFN_DIM = 2816
CONV_W = 3
NORM_EPS = 1e-6
GN_EPS = 64e-5

LANES = 128
CHUNK = 64
VMEM_LIMIT = 56 * 1024 * 1024


def _cparams(sem):
    return pltpu.CompilerParams(dimension_semantics=sem, vmem_limit_bytes=VMEM_LIMIT)


def _const_spec(shape):
    nd = len(shape)
    return pl.BlockSpec(shape, lambda *_: (0,) * nd, pipeline_mode=pl.Buffered(1))


def _rmsnorm(x, g):
    return x * lax.rsqrt(jnp.mean(x * x, axis=-1, keepdims=True) + NORM_EPS) * g


def _sigmoid(x):
    return 1.0 / (1.0 + jnp.exp(-x))


def _dot(a, b):
    return jnp.dot(a, b, preferred_element_type=F32)


def _split3(x):
    x1 = x.astype(BF16)
    r1 = x - x1.astype(F32)
    x2 = r1.astype(BF16)
    x3 = (r1 - x2.astype(F32)).astype(BF16)
    return x1, x2, x3


def _head_sum(x, ones_bd):
    x1 = x.astype(BF16)
    x2 = (x - x1.astype(F32)).astype(BF16)
    return _dot(x1, ones_bd) + _dot(x2, ones_bd)


def _rope_table_kernel(pos_ref, cos_ref, sin_ref):
    half = HEAD_DIM // 2
    lane = lax.broadcasted_iota(jnp.int32, (1, LANES), 1)
    fidx = (lane & (half - 1)).astype(F32)
    inv_freq = jnp.power(ROPE_THETA, -fidx / half)
    ang = pos_ref[...].astype(F32) * inv_freq
    sign = jnp.where((lane & half) == 0, -1.0, 1.0)
    cos_ref[...] = jnp.cos(ang)
    sin_ref[...] = jnp.sin(ang) * sign


def _rope_table(pos, tm=512):
    n = pos.shape[0]
    return pl.pallas_call(
        _rope_table_kernel,
        grid=(n // tm,),
        in_specs=[pl.BlockSpec((tm, 1), lambda i: (i, 0))],
        out_specs=[pl.BlockSpec((tm, LANES), lambda i: (i, 0))] * 2,
        out_shape=[jax.ShapeDtypeStruct((n, LANES), F32)] * 2,
        compiler_params=_cparams(("parallel",)),
        name="rope_table",
    )(pos)


def _rope(t, cos, sin):
    w = t.shape[1]
    reps = w // LANES
    half = HEAD_DIM // 2
    cos_w = jnp.concatenate([cos] * reps, axis=1)
    sin_w = jnp.concatenate([sin] * reps, axis=1)
    lane = lax.broadcasted_iota(jnp.int32, t.shape, 1)
    swapped = jnp.where((lane & half) == 0, pltpu.roll(t, w - half, 1), pltpu.roll(t, half, 1))
    return t * cos_w + swapped * sin_w


def _inproj_kernel(x_ref, g_ref, cos_ref, sin_ref, wqkv_ref, wrkv_ref, wlora_ref, wgate_ref,
                   q_ref, k_ref, v_ref, zrkv_ref, zlora_ref, gates_ref):
    h = _rmsnorm(x_ref[...], g_ref[...]).astype(BF16)
    cos = cos_ref[...]
    sin = sin_ref[...]
    qkv = _dot(h, wqkv_ref[...])
    q = _rope(qkv[:, :ATT_Q], cos, sin) * (HEAD_DIM ** -0.5)
    k = _rope(qkv[:, ATT_Q:ATT_Q + ATT_KV], cos, sin)
    q_ref[...] = q.astype(BF16)
    k_ref[...] = k.astype(BF16)
    v_ref[...] = qkv[:, ATT_Q + ATT_KV:].astype(BF16)
    zrkv_ref[...] = _dot(h, wrkv_ref[...])
    zlora_ref[...] = _dot(h, wlora_ref[...])
    gates_ref[...] = _sigmoid(_dot(h, wgate_ref[...])).astype(BF16)


def _inproj(x, g, cos, sin, wqkv, wrkv, wlora, wgate, tm=256):
    n = x.shape[0]
    row = lambda w: pl.BlockSpec((tm, w), lambda i: (i, 0))
    return pl.pallas_call(
        _inproj_kernel,
        grid=(n // tm,),
        in_specs=[row(D_MODEL), _const_spec((1, D_MODEL)), row(LANES), row(LANES),
                  _const_spec(wqkv.shape), _const_spec(wrkv.shape), _const_spec(wlora.shape),
                  _const_spec(wgate.shape)],
        out_specs=[row(ATT_Q), row(ATT_KV), row(ATT_KV), row(3 * RW_DIM), row(LORA_PAD), row(2 * D_MODEL)],
        out_shape=[jax.ShapeDtypeStruct((n, ATT_Q), BF16), jax.ShapeDtypeStruct((n, ATT_KV), BF16),
                   jax.ShapeDtypeStruct((n, ATT_KV), BF16), jax.ShapeDtypeStruct((n, 3 * RW_DIM), F32),
                   jax.ShapeDtypeStruct((n, LORA_PAD), F32), jax.ShapeDtypeStruct((n, 2 * D_MODEL), BF16)],
        compiler_params=_cparams(("parallel",)),
        name="inproj",
    )(x, g, cos, sin, wqkv, wrkv, wlora, wgate)


def _attn_kernel(q_ref, kc_ref, kp_ref, vc_ref, vp_ref, sink_ref, wo_ref, o_ref):
    n = pl.program_id(1)
    q = q_ref[...]
    kcat = jnp.concatenate([kp_ref[...], kc_ref[...]], axis=0)
    vcat = jnp.concatenate([vp_ref[...], vc_ref[...]], axis=0)
    qi = lax.broadcasted_iota(jnp.int32, (BLOCK, 2 * BLOCK), 0)
    kj = lax.broadcasted_iota(jnp.int32, (BLOCK, 2 * BLOCK), 1)
    dist = qi + BLOCK - kj
    mask = (dist >= 0) & (dist < WINDOW) & ((kj >= BLOCK) | (n > 0))
    outs = []
    for h in range(N_Q_HEADS):
        j = h // Q_PER_KV
        qh = q[:, h * HEAD_DIM:(h + 1) * HEAD_DIM]
        kh = kcat[:, j * HEAD_DIM:(j + 1) * HEAD_DIM]
        vh = vcat[:, j * HEAD_DIM:(j + 1) * HEAD_DIM]
        s = lax.dot_general(qh, kh, (((1,), (1,)), ((), ())), preferred_element_type=F32)
        s = jnp.where(mask, s, -jnp.inf)
        sink = sink_ref[h:h + 1, 0:1]
        m = jnp.maximum(jnp.max(s, axis=-1, keepdims=True), sink)
        p = jnp.exp(s - m)
        denom = jnp.sum(p, axis=-1, keepdims=True) + jnp.exp(sink - m)
        outs.append(_dot(p.astype(BF16), vh) / denom)
    o = jnp.concatenate(outs, axis=1).astype(BF16)
    o_ref[...] = _dot(o, wo_ref[...]).astype(o_ref.dtype)


def _attention(q, k, v, sinks_tiled, wo, batch, seq):
    nb = seq // BLOCK
    cur = lambda b, n: (b * nb + n, 0)
    prev = lambda b, n: (b * nb + jnp.maximum(n - 1, 0), 0)
    return pl.pallas_call(
        _attn_kernel,
        grid=(batch, nb),
        in_specs=[pl.BlockSpec((BLOCK, ATT_Q), cur),
                  pl.BlockSpec((BLOCK, ATT_KV), cur), pl.BlockSpec((BLOCK, ATT_KV), prev),
                  pl.BlockSpec((BLOCK, ATT_KV), cur), pl.BlockSpec((BLOCK, ATT_KV), prev),
                  _const_spec(sinks_tiled.shape), _const_spec(wo.shape)],
        out_specs=pl.BlockSpec((BLOCK, D_MODEL), cur),
        out_shape=jax.ShapeDtypeStruct((batch * seq, D_MODEL), BF16),
        compiler_params=_cparams(("parallel", "parallel")),
        name="swa_attention",
    )(q, k, k, v, v, sinks_tiled, wo)


def _shift_rows(z, prev_blk, first):
    rolled = pltpu.roll(z, 1, 0)
    row = lax.broadcasted_iota(jnp.int32, z.shape, 0)
    last = jnp.where(first, 0.0, prev_blk[7:8, :])
    return jnp.where(row == 0, last, rolled)


def _rwkv_prep_kernel(tiles_per_seq, zrkv_ref, zrkv_prev_ref, zlora_ref, zlora_prev_ref,
                      mu_rkv_ref, mu_lora_ref, wl_ref, w0_ref, a0_ref, kk_ref, ka_ref, rk_ref, ones_ref,
                      r_ref, ld_ref, k_ref, v_ref, a_ref, b_ref, g_ref, bonus_ref):
    first = (pl.program_id(0) % tiles_per_seq) == 0
    z = zrkv_ref[...]
    zs = z + (_shift_rows(z, zrkv_prev_ref[...], first) - z) * mu_rkv_ref[...]
    zl = zlora_ref[...]
    zls = zl + (_shift_rows(zl, zlora_prev_ref[...], first) - zl) * mu_lora_ref[...]
    lane = lax.broadcasted_iota(jnp.int32, zls.shape, 1)
    act = jnp.where(lane < DECAY_RANK, jnp.tanh(zls),
                    jnp.where(lane < DECAY_RANK + ICLR_RANK, zls, _sigmoid(zls)))
    lo = _dot(act.astype(BF16), wl_ref[...])
    r = zs[:, :RW_DIM]
    k = zs[:, RW_DIM:2 * RW_DIM]
    v = zs[:, 2 * RW_DIM:]
    wpre = -(w0_ref[...] + lo[:, :RW_DIM])
    softplus = jnp.maximum(wpre, 0.0) + jnp.log(1.0 + jnp.exp(-jnp.abs(wpre)))
    w = -softplus - 0.5
    iclr = _sigmoid(a0_ref[...] + lo[:, RW_DIM:2 * RW_DIM])
    ones_bd = ones_ref[...]
    kkr = k * kk_ref[...]
    kk = kkr / jnp.maximum(jnp.sqrt(_head_sum(kkr * kkr, ones_bd)), 1e-12)
    k2 = k * (1.0 + (iclr - 1.0) * ka_ref[...])
    r_ref[...] = r
    ld_ref[...] = -jnp.exp(w)
    k_ref[...] = k2
    v_ref[...] = v
    a_ref[...] = -kk
    b_ref[...] = kk * iclr
    g_ref[...] = lo[:, 2 * RW_DIM:]
    bonus_ref[...] = _head_sum(r * k2 * rk_ref[...], ones_bd) * v


def _rwkv_prep(zrkv, zlora, mu_rkv, mu_lora, wl, w0, a0, k_k, k_a, r_k, ones_bd, seq, tm=256):
    n = zrkv.shape[0]
    row = lambda w: pl.BlockSpec((tm, w), lambda i: (i, 0))
    prev8 = lambda w: pl.BlockSpec((8, w), lambda i: (jnp.maximum(i * (tm // 8) - 1, 0), 0))
    vec = lambda w: _const_spec((1, w))
    return pl.pallas_call(
        functools.partial(_rwkv_prep_kernel, seq // tm),
        grid=(n // tm,),
        in_specs=[row(3 * RW_DIM), prev8(3 * RW_DIM), row(LORA_PAD), prev8(LORA_PAD),
                  vec(3 * RW_DIM), vec(LORA_PAD), _const_spec(wl.shape),
                  vec(RW_DIM), vec(RW_DIM), vec(RW_DIM), vec(RW_DIM), vec(RW_DIM), _const_spec(ones_bd.shape)],
        out_specs=[row(RW_DIM)] * 8,
        out_shape=[jax.ShapeDtypeStruct((n, RW_DIM), F32)] * 8,
        compiler_params=_cparams(("parallel",)),
        name="rwkv_prep",
    )(zrkv, zrkv, zlora, zlora, mu_rkv, mu_lora, wl, w0, a0, k_k, k_a, r_k, ones_bd)


def _bdot(a, b):
    return _dot(a.astype(BF16), b.astype(BF16))


def _bdot_tn(a, b):
    return lax.dot_general(a.astype(BF16), b.astype(BF16), (((0,), (0,)), ((), ())), preferred_element_type=F32)


def _bdot_nt(a, b):
    return lax.dot_general(a.astype(BF16), b.astype(BF16), (((1,), (1,)), ((), ())), preferred_element_type=F32)


def _dot3(a, b):
    a1 = a.astype(BF16)
    a2 = (a - a1.astype(F32)).astype(BF16)
    b1 = b.astype(BF16)
    b2 = (b - b1.astype(F32)).astype(BF16)
    return _dot(a1, b1) + (_dot(a1, b2) + _dot(a2, b1))


def _unit_lower_inverse(a):
    c = a.shape[0]
    eye = (lax.broadcasted_iota(jnp.int32, (c, c), 0) == lax.broadcasted_iota(jnp.int32, (c, c), 1)).astype(F32)
    x = eye + a
    p = a
    steps = c.bit_length() - 1
    for j in range(1, steps):
        p = _dot3(p, p)
        x = x + _dot3(x, p)
    return x


def _rwkv_scan_kernel(chunks, r_ref, ld_ref, k_ref, v_ref, a_ref, b_ref, tril_ref, y_ref, state_ref):
    @pl.when(pl.program_id(2) == 0)
    def _():
        state_ref[...] = jnp.zeros_like(state_ref)

    c_len = CHUNK
    ld = ld_ref[...]
    l1, l2, l3 = _split3(ld)
    tril = tril_ref[...]
    cum = _dot(tril, l1) + (_dot(tril, l2) + _dot(tril, l3))
    r = r_ref[...]
    k = k_ref[...]
    v = v_ref[...]
    a = a_ref[...]
    b = b_ref[...]
    ti = lax.broadcasted_iota(jnp.int32, (c_len, c_len), 0)
    si = lax.broadcasted_iota(jnp.int32, (c_len, c_len), 1)
    strict = ti > si
    incl = ti >= si
    eye = ti == si
    for c in range(chunks):
        rows = slice(c * c_len, (c + 1) * c_len)
        cum_c = cum[rows]
        ld_c = ld[rows]
        end_c = cum_c[c_len - 1:c_len, :]
        g_incl = jnp.exp(cum_c)
        g_excl = jnp.exp(cum_c - ld_c)
        g_inv = jnp.exp(-cum_c)
        g_end = jnp.exp(end_c - cum_c)
        at = a[rows] * g_excl
        rt = r[rows] * g_incl
        bt = b[rows] * g_inv
        kt = k[rows] * g_inv
        bh = b[rows] * g_end
        kh = k[rows] * g_end
        gam_end = jnp.exp(end_c)
        v_c = v[rows]
        for h in range(2):
            lanes = slice(h * RW_HEAD, (h + 1) * RW_HEAD)
            at_h, rt_h, bt_h, kt_h = at[:, lanes], rt[:, lanes], bt[:, lanes], kt[:, lanes]
            bh_h, kh_h, v_h = bh[:, lanes], kh[:, lanes], v_c[:, lanes]
            a_ab = jnp.where(strict, _bdot_nt(at_h, bt_h), 0.0)
            a_ak = jnp.where(strict, _bdot_nt(at_h, kt_h), 0.0)
            a_rb = jnp.where(incl, _bdot_nt(rt_h, bt_h), 0.0)
            a_rk = jnp.where(incl, _bdot_nt(rt_h, kt_h), 0.0)
            t_inv = _unit_lower_inverse(a_ab)
            w1 = _bdot(t_inv, at_h)
            w2 = _bdot(t_inv, _bdot(a_ak, v_h))
            q_hat = rt_h + _bdot(a_rb, w1)
            y0 = _bdot(a_rb, w2) + _bdot(a_rk, v_h)
            mt = _bdot_tn(w1, bh_h) + jnp.where(eye, gam_end[:, lanes], 0.0)
            nt = _bdot_tn(w2, bh_h) + _bdot_tn(v_h, kh_h)
            s = state_ref[h]
            y_ref[rows, lanes] = _bdot_nt(q_hat, s) + y0
            state_ref[h] = _bdot(s, mt) + nt


def _rwkv_scan(r, ld, k, v, a, b, batch, seq, chunks=4):
    tc = chunks * CHUNK
    steps = seq // tc
    n = r.shape[0]
    idx = lambda bb, hp, t: (bb * steps + t, hp)
    blk = pl.BlockSpec((tc, LANES), idx)
    ii = jnp.arange(tc)
    tril = ((ii[:, None] >= ii[None, :]) & (ii[:, None] // CHUNK == ii[None, :] // CHUNK)).astype(BF16)
    return pl.pallas_call(
        functools.partial(_rwkv_scan_kernel, chunks),
        grid=(batch, RW_DIM // LANES, steps),
        in_specs=[blk] * 6 + [_const_spec((tc, tc))],
        out_specs=blk,
        out_shape=jax.ShapeDtypeStruct((n, RW_DIM), F32),
        scratch_shapes=[pltpu.VMEM((2, RW_HEAD, RW_HEAD), F32)],
        compiler_params=_cparams(("parallel", "parallel", "arbitrary")),
        name="rwkv_scan",
    )(r, ld, k, v, a, b, tril)


def _merge_kernel(x_ref, y_ref, g_ref, bonus_ref, yatt_ref, gates_ref, gnw_ref, gnb_ref, ones_ref,
                  worw_ref, wout_ref, npost_ref, o_ref):
    ones_bd = ones_ref[...]
    y = y_ref[...]
    mean = _head_sum(y, ones_bd) * (1.0 / RW_HEAD)
    d = y - mean
    var = _head_sum(d * d, ones_bd) * (1.0 / RW_HEAD)
    yn = d * lax.rsqrt(var + GN_EPS) * gnw_ref[...] + gnb_ref[...]
    rw_out = ((yn + bonus_ref[...]) * g_ref[...]).astype(BF16)
    y_rw = _dot(rw_out, worw_ref[...])
    gates = gates_ref[...].astype(F32)
    mixed = gates[:, :D_MODEL] * yatt_ref[...].astype(F32) + gates[:, D_MODEL:] * y_rw
    out = _dot(mixed.astype(BF16), wout_ref[...])
    o_ref[...] = x_ref[...] + _rmsnorm(out, npost_ref[...])


def _merge(x, y, g, bonus, yatt, gates, gn_w, gn_b, ones_bd, w_o_rw, w_out, norm_post, tm=256):
    n = x.shape[0]
    row = lambda w: pl.BlockSpec((tm, w), lambda i: (i, 0))
    vec = _const_spec((1, D_MODEL))
    mat = _const_spec((D_MODEL, D_MODEL))
    return pl.pallas_call(
        _merge_kernel,
        grid=(n // tm,),
        in_specs=[row(D_MODEL)] * 5 + [row(2 * D_MODEL), vec, vec, mat, mat, mat, vec],
        out_specs=row(D_MODEL),
        out_shape=jax.ShapeDtypeStruct((n, D_MODEL), F32),
        compiler_params=_cparams(("parallel",)),
        name="mixer_merge",
    )(x, y, g, bonus, yatt, gates, gn_w, gn_b, ones_bd, w_o_rw, w_out, norm_post)


def _gelu_tanh(x):
    return 0.5 * x * (1.0 + jnp.tanh(0.7978845608028654 * (x + 0.044715 * (x * x * x))))


def _ffn_kernel(tiles_per_seq, tm, x_ref, p_ref, npre_ref, wup_ref, cw_ref, cb_ref, wdown_ref, npost_ref,
                wple_ref, wpg_ref, nple_ref, o_ref, ubuf_ref):
    first = (pl.program_id(0) % tiles_per_seq) == 0

    @pl.when(first)
    def _():
        ubuf_ref[0:8, :] = jnp.zeros((8, 2 * FFN_DIM), F32)

    @pl.when(jnp.logical_not(first))
    def _():
        ubuf_ref[0:8, :] = ubuf_ref[tm:tm + 8, :]

    x = x_ref[...]
    h = _rmsnorm(x, npre_ref[...]).astype(BF16)
    ubuf_ref[8:tm + 8, :] = _dot(h, wup_ref[...])
    cw = cw_ref[...]
    u = (ubuf_ref[6:tm + 6, :] * cw[0:1, :] + ubuf_ref[7:tm + 7, :] * cw[1:2, :]
         + ubuf_ref[8:tm + 8, :] * cw[2:3, :] + cb_ref[...])
    act = (_gelu_tanh(u[:, :FFN_DIM]) * u[:, FFN_DIM:]).astype(BF16)
    x = x + _rmsnorm(_dot(act, wdown_ref[...]), npost_ref[...])
    e = _dot(p_ref[...].astype(BF16), wple_ref[...])
    gate = _sigmoid(_dot(x.astype(BF16), wpg_ref[...]))
    o_ref[...] = x + _rmsnorm(gate * e, nple_ref[...])


def _ffn_ple(x, p, norm_pre, w_up, conv_w8, conv_b, w_down, norm_post, w_ple, w_pg, norm_ple, seq, tm=256):
    n = x.shape[0]
    row = lambda w: pl.BlockSpec((tm, w), lambda i: (i, 0))
    vec = _const_spec((1, D_MODEL))
    return pl.pallas_call(
        functools.partial(_ffn_kernel, seq // tm, tm),
        grid=(n // tm,),
        in_specs=[row(D_MODEL), row(PLE_DIM), vec, _const_spec(w_up.shape), _const_spec(conv_w8.shape),
                  _const_spec((1, 2 * FFN_DIM)), _const_spec(w_down.shape), vec, _const_spec(w_ple.shape),
                  _const_spec(w_pg.shape), vec],
        out_specs=row(D_MODEL),
        out_shape=jax.ShapeDtypeStruct((n, D_MODEL), F32),
        scratch_shapes=[pltpu.VMEM((tm + 8, 2 * FFN_DIM), F32)],
        compiler_params=_cparams(("arbitrary",)),
        name="ffn_ple",
    )(x, p, norm_pre, w_up, conv_w8, conv_b, w_down, norm_post, w_ple, w_pg, norm_ple)


def _layer(x, p_i, cos, sin, ones_bd, batch, seq, norm_mix_pre, w_in, att_sinks, w_o_att, rw_mu, rw_w0, rw_w2,
           rw_a0, rw_a2, rw_g2, rw_k_k, rw_k_a, rw_r_k, rw_gn_w, rw_gn_b, w_o_rw, w_out, norm_mix_post,
           norm_ffn_pre, w_up, conv_w, conv_b, w_down, norm_ffn_post, w_ple, w_ple_gate, norm_ple):
    vec = lambda t: t.reshape(1, -1)
    o_rw = ATT_Q + 2 * ATT_KV
    o_lora = o_rw + 3 * RW_DIM
    o_gate = o_rw + RW_SHIFT_COLS
    wqkv = w_in[:, :o_rw].astype(BF16)
    wrkv = w_in[:, o_rw:o_lora].astype(BF16)
    wlora = jnp.pad(w_in[:, o_lora:o_gate], ((0, 0), (0, LORA_PAD - LORA_DIM))).astype(BF16)
    wgate = w_in[:, o_gate:].astype(BF16)
    q, k, v, zrkv, zlora, gates = _inproj(x, vec(norm_mix_pre), cos, sin, wqkv, wrkv, wlora, wgate)

    sinks_tiled = jnp.broadcast_to(att_sinks.reshape(N_Q_HEADS, 1), (N_Q_HEADS, LANES))
    y_att = _attention(q, k, v, sinks_tiled, w_o_att.astype(BF16), batch, seq)

    mu_rkv = vec(rw_mu[:3 * RW_DIM])
    mu_lora = vec(jnp.pad(rw_mu[3 * RW_DIM:], (0, LORA_PAD - LORA_DIM)))
    wl = jnp.zeros((LORA_PAD, 3 * RW_DIM), F32)
    wl = wl.at[:DECAY_RANK, :RW_DIM].set(rw_w2)
    wl = wl.at[DECAY_RANK:DECAY_RANK + ICLR_RANK, RW_DIM:2 * RW_DIM].set(rw_a2)
    wl = wl.at[DECAY_RANK + ICLR_RANK:LORA_DIM, 2 * RW_DIM:].set(rw_g2)
    r, ld, k2, v2, a, b, g, bonus = _rwkv_prep(
        zrkv, zlora, mu_rkv, mu_lora, wl.astype(BF16), vec(rw_w0), vec(rw_a0), vec(rw_k_k), vec(rw_k_a),
        vec(rw_r_k), ones_bd, seq)
    y = _rwkv_scan(r, ld, k2, v2, a, b, batch, seq)

    x = _merge(x, y, g, bonus, y_att, gates, vec(rw_gn_w), vec(rw_gn_b), ones_bd, w_o_rw.astype(BF16),
               w_out.astype(BF16), vec(norm_mix_post))

    conv_w8 = jnp.pad(conv_w, ((0, 8 - CONV_W), (0, 0)))
    return _ffn_ple(x, p_i, vec(norm_ffn_pre), w_up.astype(BF16), conv_w8, vec(conv_b), w_down.astype(BF16),
                    vec(norm_ffn_post), w_ple.astype(BF16), w_ple_gate.astype(BF16), vec(norm_ple), seq)


def kernel(x, p, positions, norm_mix_pre, w_in, att_sinks, w_o_att, rw_mu, rw_w0, rw_w2, rw_a0, rw_a2, rw_g2, rw_k_k, rw_k_a, rw_r_k, rw_gn_w, rw_gn_b, w_o_rw, w_out, norm_mix_post, norm_ffn_pre, w_up, conv_w, conv_b, w_down, norm_ffn_post, w_ple, w_ple_gate, norm_ple):
    batch, seq, _ = x.shape
    depth = p.shape[0]
    n = batch * seq
    assert seq % 256 == 0 and x.shape[2] == D_MODEL
    cos, sin = _rope_table(positions.reshape(n, 1).astype(jnp.int32))
    hid = jnp.arange(RW_DIM) // RW_HEAD
    ones_bd = (hid[:, None] == hid[None, :]).astype(BF16)
    xf = x.reshape(n, D_MODEL)
    per_layer = (norm_mix_pre, w_in, att_sinks, w_o_att, rw_mu, rw_w0, rw_w2, rw_a0, rw_a2, rw_g2, rw_k_k,
                 rw_k_a, rw_r_k, rw_gn_w, rw_gn_b, w_o_rw, w_out, norm_mix_post, norm_ffn_pre, w_up, conv_w,
                 conv_b, w_down, norm_ffn_post, w_ple, w_ple_gate, norm_ple)
    for i in range(depth):
        xf = _layer(xf, p[i].reshape(n, PLE_DIM), cos, sin, ones_bd, batch, seq, *(t[i] for t in per_layer))
    return xf.reshape(batch, seq, D_MODEL)
```

```python
import functools

import jax
import jax.numpy as jnp
from jax import lax
from jax.experimental import pallas as pl
from jax.experimental.pallas import tpu as pltpu

F32 = jnp.float32
BF16 = jnp.bfloat16

D_MODEL = 1024
PLE_DIM = 256
HEAD_DIM = 64
N_Q_HEADS = 16
N_KV_HEADS = 4
Q_PER_KV = N_Q_HEADS // N_KV_HEADS
WINDOW = 128
BLOCK = 128
ROPE_THETA = 10000.0
ATT_Q = N_Q_HEADS * HEAD_DIM
ATT_KV = N_KV_HEADS * HEAD_DIM
RW_HEADS = 16
RW_HEAD = 64
RW_DIM = RW_HEADS * RW_HEAD
DECAY_RANK = 64
ICLR_RANK = 64
GATE_RANK = 160
LORA_DIM = DECAY_RANK + ICLR_RANK + GATE_RANK
LORA_PAD = 384
RW_SHIFT_COLS = 3 * RW_DIM + LORA_DIM
FFN_DIM = 2816
CONV_W = 3
NORM_EPS = 1e-6
GN_EPS = 64e-5

LANES = 128
CHUNK = 64
VMEM_LIMIT = 56 * 1024 * 1024


def _cparams(sem):
    return pltpu.CompilerParams(dimension_semantics=sem, vmem_limit_bytes=VMEM_LIMIT)


def _const_spec(shape):
    nd = len(shape)
    return pl.BlockSpec(shape, lambda *_: (0,) * nd, pipeline_mode=pl.Buffered(1))


def _rmsnorm(x, g):
    return x * lax.rsqrt(jnp.mean(x * x, axis=-1, keepdims=True) + NORM_EPS) * g


def _sigmoid(x):
    return 1.0 / (1.0 + jnp.exp(-x))


def _dot(a, b):
    return jnp.dot(a, b, preferred_element_type=F32)


def _split3(x):
    x1 = x.astype(BF16)
    r1 = x - x1.astype(F32)
    x2 = r1.astype(BF16)
    x3 = (r1 - x2.astype(F32)).astype(BF16)
    return x1, x2, x3


def _head_sum(x, ones_bd):
    x1 = x.astype(BF16)
    x2 = (x - x1.astype(F32)).astype(BF16)
    return _dot(x1, ones_bd) + _dot(x2, ones_bd)


def _rope_table_kernel(pos_ref, cos_ref, sin_ref):
    half = HEAD_DIM // 2
    lane = lax.broadcasted_iota(jnp.int32, (1, LANES), 1)
    fidx = (lane & (half - 1)).astype(F32)
    inv_freq = jnp.power(ROPE_THETA, -fidx / half)
    ang = pos_ref[...].astype(F32) * inv_freq
    sign = jnp.where((lane & half) == 0, -1.0, 1.0)
    cos_ref[...] = jnp.cos(ang)
    sin_ref[...] = jnp.sin(ang) * sign


def _rope_table(pos, tm=512):
    n = pos.shape[0]
    return pl.pallas_call(
        _rope_table_kernel,
        grid=(n // tm,),
        in_specs=[pl.BlockSpec((tm, 1), lambda i: (i, 0))],
        out_specs=[pl.BlockSpec((tm, LANES), lambda i: (i, 0))] * 2,
        out_shape=[jax.ShapeDtypeStruct((n, LANES), F32)] * 2,
        compiler_params=_cparams(("parallel",)),
        name="rope_table",
    )(pos)


def _rope(t, cos, sin):
    w = t.shape[1]
    reps = w // LANES
    half = HEAD_DIM // 2
    cos_w = jnp.concatenate([cos] * reps, axis=1)
    sin_w = jnp.concatenate([sin] * reps, axis=1)
    lane = lax.broadcasted_iota(jnp.int32, t.shape, 1)
    swapped = jnp.where((lane & half) == 0, pltpu.roll(t, w - half, 1), pltpu.roll(t, half, 1))
    return t * cos_w + swapped * sin_w


def _inproj_kernel(x_ref, g_ref, cos_ref, sin_ref, wqkv_ref, wrkv_ref, wlora_ref, wgate_ref,
                   q_ref, k_ref, v_ref, zrkv_ref, zlora_ref, gates_ref):
    h = _rmsnorm(x_ref[...], g_ref[...]).astype(BF16)
    cos = cos_ref[...]
    sin = sin_ref[...]
    qkv = _dot(h, wqkv_ref[...])
    q = _rope(qkv[:, :ATT_Q], cos, sin) * (HEAD_DIM ** -0.5)
    k = _rope(qkv[:, ATT_Q:ATT_Q + ATT_KV], cos, sin)
    q_ref[...] = q.astype(BF16)
    k_ref[...] = k.astype(BF16)
    v_ref[...] = qkv[:, ATT_Q + ATT_KV:].astype(BF16)
    zrkv_ref[...] = _dot(h, wrkv_ref[...])
    zlora_ref[...] = _dot(h, wlora_ref[...])
    gates_ref[...] = _sigmoid(_dot(h, wgate_ref[...])).astype(BF16)


def _inproj(x, g, cos, sin, wqkv, wrkv, wlora, wgate, tm=256):
    n = x.shape[0]
    row = lambda w: pl.BlockSpec((tm, w), lambda i: (i, 0))
    return pl.pallas_call(
        _inproj_kernel,
        grid=(n // tm,),
        in_specs=[row(D_MODEL), _const_spec((1, D_MODEL)), row(LANES), row(LANES),
                  _const_spec(wqkv.shape), _const_spec(wrkv.shape), _const_spec(wlora.shape),
                  _const_spec(wgate.shape)],
        out_specs=[row(ATT_Q), row(ATT_KV), row(ATT_KV), row(3 * RW_DIM), row(LORA_PAD), row(2 * D_MODEL)],
        out_shape=[jax.ShapeDtypeStruct((n, ATT_Q), BF16), jax.ShapeDtypeStruct((n, ATT_KV), BF16),
                   jax.ShapeDtypeStruct((n, ATT_KV), BF16), jax.ShapeDtypeStruct((n, 3 * RW_DIM), F32),
                   jax.ShapeDtypeStruct((n, LORA_PAD), F32), jax.ShapeDtypeStruct((n, 2 * D_MODEL), BF16)],
        compiler_params=_cparams(("parallel",)),
        name="inproj",
    )(x, g, cos, sin, wqkv, wrkv, wlora, wgate)


def _attn_kernel(q_ref, kc_ref, kp_ref, vc_ref, vp_ref, sink_ref, wo_ref, o_ref):
    n = pl.program_id(1)
    q = q_ref[...]
    kcat = jnp.concatenate([kp_ref[...], kc_ref[...]], axis=0)
    vcat = jnp.concatenate([vp_ref[...], vc_ref[...]], axis=0)
    qi = lax.broadcasted_iota(jnp.int32, (BLOCK, 2 * BLOCK), 0)
    kj = lax.broadcasted_iota(jnp.int32, (BLOCK, 2 * BLOCK), 1)
    dist = qi + BLOCK - kj
    mask = (dist >= 0) & (dist < WINDOW) & ((kj >= BLOCK) | (n > 0))
    outs = []
    for h in range(N_Q_HEADS):
        j = h // Q_PER_KV
        qh = q[:, h * HEAD_DIM:(h + 1) * HEAD_DIM]
        kh = kcat[:, j * HEAD_DIM:(j + 1) * HEAD_DIM]
        vh = vcat[:, j * HEAD_DIM:(j + 1) * HEAD_DIM]
        s = lax.dot_general(qh, kh, (((1,), (1,)), ((), ())), preferred_element_type=F32)
        s = jnp.where(mask, s, -jnp.inf)
        sink = sink_ref[h:h + 1, 0:1]
        m = jnp.maximum(jnp.max(s, axis=-1, keepdims=True), sink)
        p = jnp.exp(s - m)
        denom = jnp.sum(p, axis=-1, keepdims=True) + jnp.exp(sink - m)
        outs.append(_dot(p.astype(BF16), vh) / denom)
    o = jnp.concatenate(outs, axis=1).astype(BF16)
    o_ref[...] = _dot(o, wo_ref[...]).astype(o_ref.dtype)


def _attention(q, k, v, sinks_tiled, wo, batch, seq):
    nb = seq // BLOCK
    cur = lambda b, n: (b * nb + n, 0)
    prev = lambda b, n: (b * nb + jnp.maximum(n - 1, 0), 0)
    return pl.pallas_call(
        _attn_kernel,
        grid=(batch, nb),
        in_specs=[pl.BlockSpec((BLOCK, ATT_Q), cur),
                  pl.BlockSpec((BLOCK, ATT_KV), cur), pl.BlockSpec((BLOCK, ATT_KV), prev),
                  pl.BlockSpec((BLOCK, ATT_KV), cur), pl.BlockSpec((BLOCK, ATT_KV), prev),
                  _const_spec(sinks_tiled.shape), _const_spec(wo.shape)],
        out_specs=pl.BlockSpec((BLOCK, D_MODEL), cur),
        out_shape=jax.ShapeDtypeStruct((batch * seq, D_MODEL), BF16),
        compiler_params=_cparams(("parallel", "parallel")),
        name="swa_attention",
    )(q, k, k, v, v, sinks_tiled, wo)


def _shift_rows(z, prev_blk, first):
    rolled = pltpu.roll(z, 1, 0)
    row = lax.broadcasted_iota(jnp.int32, z.shape, 0)
    last = jnp.where(first, 0.0, prev_blk[7:8, :])
    return jnp.where(row == 0, last, rolled)


def _rwkv_prep_kernel(tiles_per_seq, zrkv_ref, zrkv_prev_ref, zlora_ref, zlora_prev_ref,
                      mu_rkv_ref, mu_lora_ref, wl_ref, w0_ref, a0_ref, kk_ref, ka_ref, rk_ref, ones_ref,
                      r_ref, ld_ref, k_ref, v_ref, a_ref, b_ref, g_ref, bonus_ref):
    first = (pl.program_id(0) % tiles_per_seq) == 0
    z = zrkv_ref[...]
    zs = z + (_shift_rows(z, zrkv_prev_ref[...], first) - z) * mu_rkv_ref[...]
    zl = zlora_ref[...]
    zls = zl + (_shift_rows(zl, zlora_prev_ref[...], first) - zl) * mu_lora_ref[...]
    lane = lax.broadcasted_iota(jnp.int32, zls.shape, 1)
    act = jnp.where(lane < DECAY_RANK, jnp.tanh(zls),
                    jnp.where(lane < DECAY_RANK + ICLR_RANK, zls, _sigmoid(zls)))
    lo = _dot(act.astype(BF16), wl_ref[...])
    r = zs[:, :RW_DIM]
    k = zs[:, RW_DIM:2 * RW_DIM]
    v = zs[:, 2 * RW_DIM:]
    wpre = -(w0_ref[...] + lo[:, :RW_DIM])
    softplus = jnp.maximum(wpre, 0.0) + jnp.log(1.0 + jnp.exp(-jnp.abs(wpre)))
    w = -softplus - 0.5
    iclr = _sigmoid(a0_ref[...] + lo[:, RW_DIM:2 * RW_DIM])
    ones_bd = ones_ref[...]
    kkr = k * kk_ref[...]
    kk = kkr / jnp.maximum(jnp.sqrt(_head_sum(kkr * kkr, ones_bd)), 1e-12)
    k2 = k * (1.0 + (iclr - 1.0) * ka_ref[...])
    r_ref[...] = r
    ld_ref[...] = -jnp.exp(w)
    k_ref[...] = k2
    v_ref[...] = v
    a_ref[...] = -kk
    b_ref[...] = kk * iclr
    g_ref[...] = lo[:, 2 * RW_DIM:]
    bonus_ref[...] = _head_sum(r * k2 * rk_ref[...], ones_bd) * v


def _rwkv_prep(zrkv, zlora, mu_rkv, mu_lora, wl, w0, a0, k_k, k_a, r_k, ones_bd, seq, tm=256):
    n = zrkv.shape[0]
    row = lambda w: pl.BlockSpec((tm, w), lambda i: (i, 0))
    prev8 = lambda w: pl.BlockSpec((8, w), lambda i: (jnp.maximum(i * (tm // 8) - 1, 0), 0))
    vec = lambda w: _const_spec((1, w))
    return pl.pallas_call(
        functools.partial(_rwkv_prep_kernel, seq // tm),
        grid=(n // tm,),
        in_specs=[row(3 * RW_DIM), prev8(3 * RW_DIM), row(LORA_PAD), prev8(LORA_PAD),
                  vec(3 * RW_DIM), vec(LORA_PAD), _const_spec(wl.shape),
                  vec(RW_DIM), vec(RW_DIM), vec(RW_DIM), vec(RW_DIM), vec(RW_DIM), _const_spec(ones_bd.shape)],
        out_specs=[row(RW_DIM)] * 8,
        out_shape=[jax.ShapeDtypeStruct((n, RW_DIM), F32)] * 8,
        compiler_params=_cparams(("parallel",)),
        name="rwkv_prep",
    )(zrkv, zrkv, zlora, zlora, mu_rkv, mu_lora, wl, w0, a0, k_k, k_a, r_k, ones_bd)


PAIR = 2 * RW_HEAD
N_PAIRS = RW_DIM // PAIR


def _nt(a, b):
    return lax.dot_general(a, b, (((1,), (1,)), ((), ())), preferred_element_type=F32)


def _tn(a, b):
    return lax.dot_general(a, b, (((0,), (0,)), ((), ())), preferred_element_type=F32)


def _rwkv_scan_kernel(n_chunks, r_ref, ld_ref, k_ref, v_ref, a_ref, b_ref, tril_ref, y_ref, state_ref):
    @pl.when(pl.program_id(1) == 0)
    def _():
        state_ref[...] = jnp.zeros_like(state_ref)

    c_len = CHUNK
    row = lax.broadcasted_iota(jnp.int32, (PAIR, PAIR), 0)
    col = lax.broadcasted_iota(jnp.int32, (PAIR, PAIR), 1)
    same_head = (row // c_len) == (col // c_len)
    strict = same_head & ((row % c_len) > (col % c_len))
    incl = same_head & ((row % c_len) >= (col % c_len))
    eye = row == col
    eye_f = eye.astype(F32)
    first_head = (lax.broadcasted_iota(jnp.int32, (c_len, RW_DIM), 1) & RW_HEAD) == 0
    tril = tril_ref[...]
    pairs = range(N_PAIRS)

    def halves(x):
        return jnp.where(first_head, x, 0.0).astype(BF16), jnp.where(first_head, 0.0, x).astype(BF16)

    def bd(hv, p):
        sl = slice(p * PAIR, (p + 1) * PAIR)
        return jnp.concatenate([hv[0][:, sl], hv[1][:, sl]], axis=0)

    def chunk_body(c, carry):
        rows = pl.ds(pl.multiple_of(c * c_len, c_len), c_len)
        ld = ld_ref[rows, :]
        l1, l2, l3 = _split3(ld)
        cum = _dot(tril, l1) + (_dot(tril, l2) + _dot(tril, l3))
        end = cum[c_len - 1:c_len, :]
        g_inv = jnp.exp(-cum)
        g_end = jnp.exp(end - cum)
        gam_end = jnp.exp(end)
        a = a_ref[rows, :]
        b = b_ref[rows, :]
        k = k_ref[rows, :]
        at = halves(a * jnp.exp(cum - ld))
        rt = halves(r_ref[rows, :] * jnp.exp(cum))
        bt = halves(b * g_inv)
        kt = halves(k * g_inv)
        bh = halves(b * g_end)
        kh = halves(k * g_end)
        vv = halves(v_ref[rows, :])
        bd_at = [bd(at, p) for p in pairs]
        bd_rt = [bd(rt, p) for p in pairs]
        bd_v = [bd(vv, p) for p in pairs]

        g1 = [_nt(jnp.concatenate([bd_at[p], bd_rt[p]], axis=0),
                  jnp.concatenate([bd(bt, p), bd(kt, p)], axis=0)) for p in pairs]
        a_ab = [jnp.where(strict, g[:PAIR, :PAIR], 0.0) for g in g1]
        a_ak = [jnp.where(strict, g[:PAIR, PAIR:], 0.0).astype(BF16) for g in g1]
        a_rb = [jnp.where(incl, g[PAIR:, :PAIR], 0.0).astype(BF16) for g in g1]
        a_rk = [jnp.where(incl, g[PAIR:, PAIR:], 0.0).astype(BF16) for g in g1]

        x = [eye_f + m for m in a_ab]
        pw = [m.astype(BF16) for m in a_ab]
        pw = [_dot(m, m).astype(BF16) for m in pw]
        for _ in range(c_len.bit_length() - 3):
            res = [_dot(jnp.concatenate([xm.astype(BF16), pm], axis=0), pm) for xm, pm in zip(x, pw)]
            x = [xm + rs[:PAIR] for xm, rs in zip(x, res)]
            pw = [rs[PAIR:].astype(BF16) for rs in res]
        t_inv = [(xm + _dot(xm.astype(BF16), pm)).astype(BF16) for xm, pm in zip(x, pw)]

        av = [_dot(m, vb).astype(BF16) for m, vb in zip(a_ak, bd_v)]
        w12 = [_dot(t, jnp.concatenate([ab, avp], axis=1)).astype(BF16) for t, ab, avp in zip(t_inv, bd_at, av)]
        qy = [_dot(m, w) for m, w in zip(a_rb, w12)]
        q_hat = [(rb.astype(F32) + q[:, :PAIR]).astype(BF16) for rb, q in zip(bd_rt, qy)]
        y0 = [q[:, PAIR:] + _dot(m, vb) for q, m, vb in zip(qy, a_rk, bd_v)]
        mn = [_tn(w, bd(bh, p)) for p, w in zip(pairs, w12)]
        mt = [(g[:PAIR] + jnp.where(eye, gam_end[:, p * PAIR:(p + 1) * PAIR], 0.0)).astype(BF16)
              for p, g in zip(pairs, mn)]
        nt = [g[PAIR:] + _tn(vb, bd(kh, p)) for p, g, vb in zip(pairs, mn, bd_v)]

        for p in pairs:
            s = state_ref[p]
            sb = s.astype(BF16)
            y_bd = _nt(q_hat[p], sb) + y0[p]
            y_ref[rows, p * PAIR:(p + 1) * PAIR] = y_bd[:c_len] + y_bd[c_len:]
            state_ref[p] = _dot(sb, mt[p]) + nt[p]
        return carry

    lax.fori_loop(0, n_chunks, chunk_body, 0)


def _rwkv_scan(r, ld, k, v, a, b, batch, seq, tc=512):
    steps = seq // tc
    n = r.shape[0]
    blk = pl.BlockSpec((tc, RW_DIM), lambda bb, t: (bb * steps + t, 0))
    ii = jnp.arange(CHUNK)
    tril = (ii[:, None] >= ii[None, :]).astype(BF16)
    return pl.pallas_call(
        functools.partial(_rwkv_scan_kernel, tc // CHUNK),
        grid=(batch, steps),
        in_specs=[blk] * 6 + [_const_spec((CHUNK, CHUNK))],
        out_specs=blk,
        out_shape=jax.ShapeDtypeStruct((n, RW_DIM), F32),
        scratch_shapes=[pltpu.VMEM((N_PAIRS, PAIR, PAIR), F32)],
        compiler_params=_cparams(("parallel", "arbitrary")),
        name="rwkv_scan",
    )(r, ld, k, v, a, b, tril)


def _merge_kernel(x_ref, y_ref, g_ref, bonus_ref, yatt_ref, gates_ref, gnw_ref, gnb_ref, ones_ref,
                  worw_ref, wout_ref, npost_ref, o_ref):
    ones_bd = ones_ref[...]
    y = y_ref[...]
    mean = _head_sum(y, ones_bd) * (1.0 / RW_HEAD)
    d = y - mean
    var = _head_sum(d * d, ones_bd) * (1.0 / RW_HEAD)
    yn = d * lax.rsqrt(var + GN_EPS) * gnw_ref[...] + gnb_ref[...]
    rw_out = ((yn + bonus_ref[...]) * g_ref[...]).astype(BF16)
    y_rw = _dot(rw_out, worw_ref[...])
    gates = gates_ref[...].astype(F32)
    mixed = gates[:, :D_MODEL] * yatt_ref[...].astype(F32) + gates[:, D_MODEL:] * y_rw
    out = _dot(mixed.astype(BF16), wout_ref[...])
    o_ref[...] = x_ref[...] + _rmsnorm(out, npost_ref[...])


def _merge(x, y, g, bonus, yatt, gates, gn_w, gn_b, ones_bd, w_o_rw, w_out, norm_post, tm=256):
    n = x.shape[0]
    row = lambda w: pl.BlockSpec((tm, w), lambda i: (i, 0))
    vec = _const_spec((1, D_MODEL))
    mat = _const_spec((D_MODEL, D_MODEL))
    return pl.pallas_call(
        _merge_kernel,
        grid=(n // tm,),
        in_specs=[row(D_MODEL)] * 5 + [row(2 * D_MODEL), vec, vec, mat, mat, mat, vec],
        out_specs=row(D_MODEL),
        out_shape=jax.ShapeDtypeStruct((n, D_MODEL), F32),
        compiler_params=_cparams(("parallel",)),
        name="mixer_merge",
    )(x, y, g, bonus, yatt, gates, gn_w, gn_b, ones_bd, w_o_rw, w_out, norm_post)


def _gelu_tanh(x):
    return 0.5 * x * (1.0 + jnp.tanh(0.7978845608028654 * (x + 0.044715 * (x * x * x))))


def _ffn_kernel(tiles_per_seq, tm, x_ref, p_ref, npre_ref, wup_ref, cw_ref, cb_ref, wdown_ref, npost_ref,
                wple_ref, wpg_ref, nple_ref, o_ref, ubuf_ref):
    first = (pl.program_id(0) % tiles_per_seq) == 0

    @pl.when(first)
    def _():
        ubuf_ref[0:8, :] = jnp.zeros((8, 2 * FFN_DIM), F32)

    @pl.when(jnp.logical_not(first))
    def _():
        ubuf_ref[0:8, :] = ubuf_ref[tm:tm + 8, :]

    x = x_ref[...]
    h = _rmsnorm(x, npre_ref[...]).astype(BF16)
    ubuf_ref[8:tm + 8, :] = _dot(h, wup_ref[...])
    cw = cw_ref[...]
    u = (ubuf_ref[6:tm + 6, :] * cw[0:1, :] + ubuf_ref[7:tm + 7, :] * cw[1:2, :]
         + ubuf_ref[8:tm + 8, :] * cw[2:3, :] + cb_ref[...])
    act = (_gelu_tanh(u[:, :FFN_DIM]) * u[:, FFN_DIM:]).astype(BF16)
    x = x + _rmsnorm(_dot(act, wdown_ref[...]), npost_ref[...])
    e = _dot(p_ref[...].astype(BF16), wple_ref[...])
    gate = _sigmoid(_dot(x.astype(BF16), wpg_ref[...]))
    o_ref[...] = x + _rmsnorm(gate * e, nple_ref[...])


def _ffn_ple(x, p, norm_pre, w_up, conv_w8, conv_b, w_down, norm_post, w_ple, w_pg, norm_ple, seq, tm=256):
    n = x.shape[0]
    row = lambda w: pl.BlockSpec((tm, w), lambda i: (i, 0))
    vec = _const_spec((1, D_MODEL))
    return pl.pallas_call(
        functools.partial(_ffn_kernel, seq // tm, tm),
        grid=(n // tm,),
        in_specs=[row(D_MODEL), row(PLE_DIM), vec, _const_spec(w_up.shape), _const_spec(conv_w8.shape),
                  _const_spec((1, 2 * FFN_DIM)), _const_spec(w_down.shape), vec, _const_spec(w_ple.shape),
                  _const_spec(w_pg.shape), vec],
        out_specs=row(D_MODEL),
        out_shape=jax.ShapeDtypeStruct((n, D_MODEL), F32),
        scratch_shapes=[pltpu.VMEM((tm + 8, 2 * FFN_DIM), F32)],
        compiler_params=_cparams(("arbitrary",)),
        name="ffn_ple",
    )(x, p, norm_pre, w_up, conv_w8, conv_b, w_down, norm_post, w_ple, w_pg, norm_ple)


def _layer(x, p_i, cos, sin, ones_bd, batch, seq, norm_mix_pre, w_in, att_sinks, w_o_att, rw_mu, rw_w0, rw_w2,
           rw_a0, rw_a2, rw_g2, rw_k_k, rw_k_a, rw_r_k, rw_gn_w, rw_gn_b, w_o_rw, w_out, norm_mix_post,
           norm_ffn_pre, w_up, conv_w, conv_b, w_down, norm_ffn_post, w_ple, w_ple_gate, norm_ple):
    vec = lambda t: t.reshape(1, -1)
    o_rw = ATT_Q + 2 * ATT_KV
    o_lora = o_rw + 3 * RW_DIM
    o_gate = o_rw + RW_SHIFT_COLS
    wqkv = w_in[:, :o_rw].astype(BF16)
    wrkv = w_in[:, o_rw:o_lora].astype(BF16)
    wlora = jnp.pad(w_in[:, o_lora:o_gate], ((0, 0), (0, LORA_PAD - LORA_DIM))).astype(BF16)
    wgate = w_in[:, o_gate:].astype(BF16)
    q, k, v, zrkv, zlora, gates = _inproj(x, vec(norm_mix_pre), cos, sin, wqkv, wrkv, wlora, wgate)

    sinks_tiled = jnp.broadcast_to(att_sinks.reshape(N_Q_HEADS, 1), (N_Q_HEADS, LANES))
    y_att = _attention(q, k, v, sinks_tiled, w_o_att.astype(BF16), batch, seq)

    mu_rkv = vec(rw_mu[:3 * RW_DIM])
    mu_lora = vec(jnp.pad(rw_mu[3 * RW_DIM:], (0, LORA_PAD - LORA_DIM)))
    wl = jnp.zeros((LORA_PAD, 3 * RW_DIM), F32)
    wl = wl.at[:DECAY_RANK, :RW_DIM].set(rw_w2)
    wl = wl.at[DECAY_RANK:DECAY_RANK + ICLR_RANK, RW_DIM:2 * RW_DIM].set(rw_a2)
    wl = wl.at[DECAY_RANK + ICLR_RANK:LORA_DIM, 2 * RW_DIM:].set(rw_g2)
    r, ld, k2, v2, a, b, g, bonus = _rwkv_prep(
        zrkv, zlora, mu_rkv, mu_lora, wl.astype(BF16), vec(rw_w0), vec(rw_a0), vec(rw_k_k), vec(rw_k_a),
        vec(rw_r_k), ones_bd, seq)
    y = _rwkv_scan(r, ld, k2, v2, a, b, batch, seq)

    x = _merge(x, y, g, bonus, y_att, gates, vec(rw_gn_w), vec(rw_gn_b), ones_bd, w_o_rw.astype(BF16),
               w_out.astype(BF16), vec(norm_mix_post))

    conv_w8 = jnp.pad(conv_w, ((0, 8 - CONV_W), (0, 0)))
    return _ffn_ple(x, p_i, vec(norm_ffn_pre), w_up.astype(BF16), conv_w8, vec(conv_b), w_down.astype(BF16),
                    vec(norm_ffn_post), w_ple.astype(BF16), w_ple_gate.astype(BF16), vec(norm_ple), seq)


def kernel(x, p, positions, norm_mix_pre, w_in, att_sinks, w_o_att, rw_mu, rw_w0, rw_w2, rw_a0, rw_a2, rw_g2, rw_k_k, rw_k_a, rw_r_k, rw_gn_w, rw_gn_b, w_o_rw, w_out, norm_mix_post, norm_ffn_pre, w_up, conv_w, conv_b, w_down, norm_ffn_post, w_ple, w_ple_gate, norm_ple):
    batch, seq, _ = x.shape
    depth = p.shape[0]
    n = batch * seq
    assert seq % 512 == 0 and x.shape[2] == D_MODEL
    cos, sin = _rope_table(positions.reshape(n, 1).astype(jnp.int32))
    hid = jnp.arange(RW_DIM) // RW_HEAD
    ones_bd = (hid[:, None] == hid[None, :]).astype(BF16)
    xf = x.reshape(n, D_MODEL)
    per_layer = (norm_mix_pre, w_in, att_sinks, w_o_att, rw_mu, rw_w0, rw_w2, rw_a0, rw_a2, rw_g2, rw_k_k,
                 rw_k_a, rw_r_k, rw_gn_w, rw_gn_b, w_o_rw, w_out, norm_mix_post, norm_ffn_pre, w_up, conv_w,
                 conv_b, w_down, norm_ffn_post, w_ple, w_ple_gate, norm_ple)
    for i in range(depth):
        xf = _layer(xf, p[i].reshape(n, PLE_DIM), cos, sin, ones_bd, batch, seq, *(t[i] for t in per_layer))
    return xf.reshape(batch, seq, D_MODEL)
```

```python
import functools

import jax
import jax.numpy as jnp
from jax import lax
from jax.experimental import pallas as pl
from jax.experimental.pallas import tpu as pltpu

F32 = jnp.float32
BF16 = jnp.bfloat16

D_MODEL = 1024
PLE_DIM = 256
HEAD_DIM = 64
N_Q_HEADS = 16
N_KV_HEADS = 4
Q_PER_KV = N_Q_HEADS // N_KV_HEADS
WINDOW = 128
BLOCK = 128
ROPE_THETA = 10000.0
ATT_Q = N_Q_HEADS * HEAD_DIM
ATT_KV = N_KV_HEADS * HEAD_DIM
RW_HEADS = 16
RW_HEAD = 64
RW_DIM = RW_HEADS * RW_HEAD
DECAY_RANK = 64
ICLR_RANK = 64
GATE_RANK = 160
LORA_DIM = DECAY_RANK + ICLR_RANK + GATE_RANK
LORA_PAD = 384
RW_SHIFT_COLS = 3 * RW_DIM + LORA_DIM
FFN_DIM = 2816
CONV_W = 3
NORM_EPS = 1e-6
GN_EPS = 64e-5

LANES = 128
CHUNK = 64
VMEM_LIMIT = 56 * 1024 * 1024


def _cparams(sem):
    return pltpu.CompilerParams(dimension_semantics=sem, vmem_limit_bytes=VMEM_LIMIT)


def _const_spec(shape):
    nd = len(shape)
    return pl.BlockSpec(shape, lambda *_: (0,) * nd, pipeline_mode=pl.Buffered(1))


def _rmsnorm(x, g):
    return x * lax.rsqrt(jnp.mean(x * x, axis=-1, keepdims=True) + NORM_EPS) * g


def _sigmoid(x):
    return 1.0 / (1.0 + jnp.exp(-x))


def _dot(a, b):
    return jnp.dot(a, b, preferred_element_type=F32)


def _split3(x):
    x1 = x.astype(BF16)
    r1 = x - x1.astype(F32)
    x2 = r1.astype(BF16)
    x3 = (r1 - x2.astype(F32)).astype(BF16)
    return x1, x2, x3


def _head_sum(x, ones_bd):
    x1 = x.astype(BF16)
    x2 = (x - x1.astype(F32)).astype(BF16)
    w = ones_bd.shape[0]
    outs = []
    for i in range(x.shape[1] // w):
        sl = slice(i * w, (i + 1) * w)
        outs.append(_dot(x1[:, sl], ones_bd) + _dot(x2[:, sl], ones_bd))
    return jnp.concatenate(outs, axis=1)


def _rope_table_kernel(pos_ref, cos_ref, sin_ref):
    half = HEAD_DIM // 2
    lane = lax.broadcasted_iota(jnp.int32, (1, LANES), 1)
    fidx = (lane & (half - 1)).astype(F32)
    inv_freq = jnp.power(ROPE_THETA, -fidx / half)
    ang = pos_ref[...].astype(F32) * inv_freq
    sign = jnp.where((lane & half) == 0, -1.0, 1.0)
    cos_ref[...] = jnp.cos(ang)
    sin_ref[...] = jnp.sin(ang) * sign


def _rope_table(pos, tm=512):
    n = pos.shape[0]
    return pl.pallas_call(
        _rope_table_kernel,
        grid=(n // tm,),
        in_specs=[pl.BlockSpec((tm, 1), lambda i: (i, 0))],
        out_specs=[pl.BlockSpec((tm, LANES), lambda i: (i, 0))] * 2,
        out_shape=[jax.ShapeDtypeStruct((n, LANES), F32)] * 2,
        compiler_params=_cparams(("parallel",)),
        name="rope_table",
    )(pos)


def _rope(t, cos, sin):
    w = t.shape[1]
    reps = w // LANES
    half = HEAD_DIM // 2
    cos_w = jnp.concatenate([cos] * reps, axis=1)
    sin_w = jnp.concatenate([sin] * reps, axis=1)
    lane = lax.broadcasted_iota(jnp.int32, t.shape, 1)
    swapped = jnp.where((lane & half) == 0, pltpu.roll(t, w - half, 1), pltpu.roll(t, half, 1))
    return t * cos_w + swapped * sin_w


def _inproj_kernel(x_ref, g_ref, cos_ref, sin_ref, wqkv_ref, wrkv_ref, wlora_ref, wgate_ref,
                   q_ref, k_ref, v_ref, zrkv_ref, zlora_ref, gates_ref):
    h = _rmsnorm(x_ref[...], g_ref[...]).astype(BF16)
    cos = cos_ref[...]
    sin = sin_ref[...]
    qkv = _dot(h, wqkv_ref[...])
    q = _rope(qkv[:, :ATT_Q], cos, sin) * (HEAD_DIM ** -0.5)
    k = _rope(qkv[:, ATT_Q:2 * ATT_Q], cos, sin)
    q_ref[...] = q.astype(BF16)
    k_ref[...] = k.astype(BF16)
    v_ref[...] = qkv[:, 2 * ATT_Q:].astype(BF16)
    zrkv_ref[...] = _dot(h, wrkv_ref[...]).astype(BF16)
    zlora_ref[...] = _dot(h, wlora_ref[...])
    gates_ref[...] = _sigmoid(_dot(h, wgate_ref[...])).astype(BF16)


def _inproj(x, g, cos, sin, wqkv, wrkv, wlora, wgate, tm=256):
    n = x.shape[0]
    row = lambda w: pl.BlockSpec((tm, w), lambda i: (i, 0))
    return pl.pallas_call(
        _inproj_kernel,
        grid=(n // tm,),
        in_specs=[row(D_MODEL), _const_spec((1, D_MODEL)), row(LANES), row(LANES),
                  _const_spec(wqkv.shape), _const_spec(wrkv.shape), _const_spec(wlora.shape),
                  _const_spec(wgate.shape)],
        out_specs=[row(ATT_Q), row(ATT_Q), row(ATT_Q), row(3 * RW_DIM), row(LORA_PAD), row(2 * D_MODEL)],
        out_shape=[jax.ShapeDtypeStruct((n, ATT_Q), BF16), jax.ShapeDtypeStruct((n, ATT_Q), BF16),
                   jax.ShapeDtypeStruct((n, ATT_Q), BF16), jax.ShapeDtypeStruct((n, 3 * RW_DIM), BF16),
                   jax.ShapeDtypeStruct((n, LORA_PAD), F32), jax.ShapeDtypeStruct((n, 2 * D_MODEL), BF16)],
        compiler_params=_cparams(("parallel",)),
        name="inproj",
    )(x, g, cos, sin, wqkv, wrkv, wlora, wgate)


GROUP_W = Q_PER_KV * HEAD_DIM
KEYS = 2 * BLOCK


def _attn_kernel(q_ref, kc_ref, kp_ref, vc_ref, vp_ref, bias_ref, bdmask_ref, sink_ref, wo_ref, o_ref):
    q = q_ref[...]
    kcat = jnp.concatenate([kp_ref[...], kc_ref[...]], axis=0)
    vcat = jnp.concatenate([vp_ref[...], vc_ref[...]], axis=0)
    bias = bias_ref[...]
    bdmask = bdmask_ref[...]
    lane = lax.broadcasted_iota(jnp.int32, (BLOCK, GROUP_W), 1)
    kv_heads = range(N_KV_HEADS)
    groups = [slice(j * GROUP_W, (j + 1) * GROUP_W) for j in kv_heads]

    kbd = [jnp.concatenate([kcat[:, sl]] * Q_PER_KV, axis=0) * bdmask for sl in groups]
    s = [_nt(q[:, sl], kb) + bias for sl, kb in zip(groups, kbd)]
    p, sink_terms = [], []
    for j in kv_heads:
        pj, ej = [], []
        for g in range(Q_PER_KV):
            sg = s[j][:, g * KEYS:(g + 1) * KEYS]
            h = j * Q_PER_KV + g
            sink = sink_ref[h:h + 1, 0:1]
            m = jnp.maximum(jnp.max(sg, axis=-1, keepdims=True), sink)
            pj.append(jnp.exp(sg - m).astype(BF16))
            ej.append(jnp.exp(sink - m))
        p.append(jnp.concatenate(pj, axis=1))
        sink_terms.append(ej)
    vbd = [jnp.concatenate([jnp.concatenate([vcat[:, sl]] * Q_PER_KV, axis=0) * bdmask, bdmask], axis=1)
           for sl in groups]
    ov = [_dot(pj, vb) for pj, vb in zip(p, vbd)]
    outs = []
    for j in kv_heads:
        e = sink_terms[j]
        sink_add = e[Q_PER_KV - 1]
        for g in reversed(range(Q_PER_KV - 1)):
            sink_add = jnp.where(lane < (g + 1) * HEAD_DIM, e[g], sink_add)
        outs.append(ov[j][:, :GROUP_W] / (ov[j][:, GROUP_W:] + sink_add))
    o = jnp.concatenate(outs, axis=1).astype(BF16)
    o_ref[...] = _dot(o, wo_ref[...]).astype(o_ref.dtype)


def _attention(q, k, v, sinks_tiled, wo, batch, seq):
    nb = seq // BLOCK
    cur = lambda b, n: (b * nb + n, 0)
    prev = lambda b, n: (b * nb + jnp.maximum(n - 1, 0), 0)
    qi = jnp.arange(BLOCK)[:, None]
    kj = jnp.arange(KEYS)[None, :]
    dist = qi + BLOCK - kj
    band = (dist >= 0) & (dist < WINDOW)
    masks = jnp.stack([band & (kj >= BLOCK), band])
    bias = jnp.tile(jnp.where(masks, 0.0, -jnp.inf).astype(F32), (1, 1, Q_PER_KV))
    member_r = jnp.arange(Q_PER_KV * KEYS)[:, None] // KEYS
    member_c = jnp.arange(GROUP_W)[None, :] // HEAD_DIM
    bdmask = (member_r == member_c).astype(BF16)
    blk = lambda imap: pl.BlockSpec((BLOCK, ATT_Q), imap)
    return pl.pallas_call(
        _attn_kernel,
        grid=(batch, nb),
        in_specs=[blk(cur), blk(cur), blk(prev), blk(cur), blk(prev),
                  pl.BlockSpec((None, BLOCK, Q_PER_KV * KEYS), lambda b, n: (jnp.minimum(n, 1), 0, 0)),
                  _const_spec(bdmask.shape), _const_spec(sinks_tiled.shape), _const_spec(wo.shape)],
        out_specs=pl.BlockSpec((BLOCK, D_MODEL), cur),
        out_shape=jax.ShapeDtypeStruct((batch * seq, D_MODEL), BF16),
        compiler_params=_cparams(("parallel", "parallel")),
        name="swa_attention",
    )(q, k, k, v, v, bias, bdmask, sinks_tiled, wo)


def _shift_rows(z, prev_blk, first):
    rolled = pltpu.roll(z, 1, 0)
    row = lax.broadcasted_iota(jnp.int32, z.shape, 0)
    rows = prev_blk.shape[0]
    last = jnp.where(first, 0.0, prev_blk[rows - 1:rows, :].astype(F32))
    return jnp.where(row == 0, last, rolled)


def _rwkv_prep_kernel(tiles_per_seq, zrkv_ref, zrkv_prev_ref, zlora_ref, zlora_prev_ref,
                      mu_rkv_ref, mu_lora_ref, wl_ref, w0_ref, a0_ref, kk_ref, ka_ref, rk_ref, ones_ref,
                      r_ref, ld_ref, k_ref, v_ref, a_ref, b_ref, g_ref, bonus_ref):
    first = (pl.program_id(0) % tiles_per_seq) == 0
    z = zrkv_ref[...].astype(F32)
    zs = z + (_shift_rows(z, zrkv_prev_ref[...], first) - z) * mu_rkv_ref[...]
    zl = zlora_ref[...]
    zls = zl + (_shift_rows(zl, zlora_prev_ref[...], first) - zl) * mu_lora_ref[...]
    lane = lax.broadcasted_iota(jnp.int32, zls.shape, 1)
    act = jnp.where(lane < DECAY_RANK, jnp.tanh(zls),
                    jnp.where(lane < DECAY_RANK + ICLR_RANK, zls, _sigmoid(zls)))
    lo = _dot(act.astype(BF16), wl_ref[...])
    r = zs[:, :RW_DIM]
    k = zs[:, RW_DIM:2 * RW_DIM]
    v = zs[:, 2 * RW_DIM:]
    wpre = -(w0_ref[...] + lo[:, :RW_DIM])
    softplus = jnp.maximum(wpre, 0.0) + jnp.log(1.0 + jnp.exp(-jnp.abs(wpre)))
    w = -softplus - 0.5
    iclr = _sigmoid(a0_ref[...] + lo[:, RW_DIM:2 * RW_DIM])
    ones_bd = ones_ref[...]
    kkr = k * kk_ref[...]
    kk = kkr / jnp.maximum(jnp.sqrt(_head_sum(kkr * kkr, ones_bd)), 1e-12)
    k2 = k * (1.0 + (iclr - 1.0) * ka_ref[...])
    r_ref[...] = r.astype(BF16)
    ld_ref[...] = -jnp.exp(w)
    k_ref[...] = k2.astype(BF16)
    v_ref[...] = v.astype(BF16)
    a_ref[...] = (-kk).astype(BF16)
    b_ref[...] = (kk * iclr).astype(BF16)
    g_ref[...] = lo[:, 2 * RW_DIM:].astype(BF16)
    bonus_ref[...] = (_head_sum(r * k2 * rk_ref[...], ones_bd) * v).astype(BF16)


def _rwkv_prep(zrkv, zlora, mu_rkv, mu_lora, wl, w0, a0, k_k, k_a, r_k, ones_bd, seq, tm=256):
    n = zrkv.shape[0]
    row = lambda w: pl.BlockSpec((tm, w), lambda i: (i, 0))
    prev = lambda rows, w: pl.BlockSpec((rows, w), lambda i: (jnp.maximum(i * (tm // rows) - 1, 0), 0))
    vec = lambda w: _const_spec((1, w))
    out_dtypes = [BF16, F32, BF16, BF16, BF16, BF16, BF16, BF16]
    return pl.pallas_call(
        functools.partial(_rwkv_prep_kernel, seq // tm),
        grid=(n // tm,),
        in_specs=[row(3 * RW_DIM), prev(16, 3 * RW_DIM), row(LORA_PAD), prev(8, LORA_PAD),
                  vec(3 * RW_DIM), vec(LORA_PAD), _const_spec(wl.shape),
                  vec(RW_DIM), vec(RW_DIM), vec(RW_DIM), vec(RW_DIM), vec(RW_DIM), _const_spec(ones_bd.shape)],
        out_specs=[row(RW_DIM)] * 8,
        out_shape=[jax.ShapeDtypeStruct((n, RW_DIM), dt) for dt in out_dtypes],
        compiler_params=_cparams(("parallel",)),
        name="rwkv_prep",
    )(zrkv, zrkv, zlora, zlora, mu_rkv, mu_lora, wl, w0, a0, k_k, k_a, r_k, ones_bd)


PAIR = 2 * RW_HEAD
N_PAIRS = RW_DIM // PAIR


def _nt(a, b):
    return lax.dot_general(a, b, (((1,), (1,)), ((), ())), preferred_element_type=F32)


def _tn(a, b):
    return lax.dot_general(a, b, (((0,), (0,)), ((), ())), preferred_element_type=F32)


def _rwkv_scan_kernel(n_chunks, r_ref, ld_ref, k_ref, v_ref, a_ref, b_ref, tril_ref, y_ref, state_ref):
    @pl.when(pl.program_id(1) == 0)
    def _():
        state_ref[...] = jnp.zeros_like(state_ref)

    c_len = CHUNK
    row = lax.broadcasted_iota(jnp.int32, (PAIR, PAIR), 0)
    col = lax.broadcasted_iota(jnp.int32, (PAIR, PAIR), 1)
    same_head = (row // c_len) == (col // c_len)
    strict = same_head & ((row % c_len) > (col % c_len))
    incl = same_head & ((row % c_len) >= (col % c_len))
    eye = row == col
    eye_f = eye.astype(F32)
    first_head = (lax.broadcasted_iota(jnp.int32, (c_len, RW_DIM), 1) & RW_HEAD) == 0
    tril = tril_ref[...]
    pairs = range(N_PAIRS)

    def halves(x):
        return jnp.where(first_head, x, 0.0).astype(BF16), jnp.where(first_head, 0.0, x).astype(BF16)

    def bd(hv, p):
        sl = slice(p * PAIR, (p + 1) * PAIR)
        return jnp.concatenate([hv[0][:, sl], hv[1][:, sl]], axis=0)

    def chunk_body(c, carry):
        rows = pl.ds(pl.multiple_of(c * c_len, c_len), c_len)
        ld = ld_ref[rows, :]
        l1, l2, l3 = _split3(ld)
        cum = _dot(tril, l1) + (_dot(tril, l2) + _dot(tril, l3))
        end = cum[c_len - 1:c_len, :]
        g_inv = jnp.exp(-cum)
        g_end = jnp.exp(end - cum)
        gam_end = jnp.exp(end)
        a = a_ref[rows, :].astype(F32)
        b = b_ref[rows, :].astype(F32)
        k = k_ref[rows, :].astype(F32)
        at = halves(a * jnp.exp(cum - ld))
        rt = halves(r_ref[rows, :].astype(F32) * jnp.exp(cum))
        bt = halves(b * g_inv)
        kt = halves(k * g_inv)
        bh = halves(b * g_end)
        kh = halves(k * g_end)
        vv = halves(v_ref[rows, :].astype(F32))
        bd_at = [bd(at, p) for p in pairs]
        bd_rt = [bd(rt, p) for p in pairs]
        bd_v = [bd(vv, p) for p in pairs]

        g1 = [_nt(jnp.concatenate([bd_at[p], bd_rt[p]], axis=0),
                  jnp.concatenate([bd(bt, p), bd(kt, p)], axis=0)) for p in pairs]
        a_ab = [jnp.where(strict, g[:PAIR, :PAIR], 0.0) for g in g1]
        a_ak = [jnp.where(strict, g[:PAIR, PAIR:], 0.0).astype(BF16) for g in g1]
        a_rb = [jnp.where(incl, g[PAIR:, :PAIR], 0.0).astype(BF16) for g in g1]
        a_rk = [jnp.where(incl, g[PAIR:, PAIR:], 0.0).astype(BF16) for g in g1]

        x = [eye_f + m for m in a_ab]
        pw = [m.astype(BF16) for m in a_ab]
        pw = [_dot(m, m).astype(BF16) for m in pw]
        for _ in range(c_len.bit_length() - 3):
            res = [_dot(jnp.concatenate([xm.astype(BF16), pm], axis=0), pm) for xm, pm in zip(x, pw)]
            x = [xm + rs[:PAIR] for xm, rs in zip(x, res)]
            pw = [rs[PAIR:].astype(BF16) for rs in res]
        t_inv = [(xm + _dot(xm.astype(BF16), pm)).astype(BF16) for xm, pm in zip(x, pw)]

        av = [_dot(m, vb).astype(BF16) for m, vb in zip(a_ak, bd_v)]
        w12 = [_dot(t, jnp.concatenate([ab, avp], axis=1)).astype(BF16) for t, ab, avp in zip(t_inv, bd_at, av)]
        qy = [_dot(m, w) for m, w in zip(a_rb, w12)]
        q_hat = [(rb.astype(F32) + q[:, :PAIR]).astype(BF16) for rb, q in zip(bd_rt, qy)]
        y0 = [q[:, PAIR:] + _dot(m, vb) for q, m, vb in zip(qy, a_rk, bd_v)]
        mn = [_tn(w, bd(bh, p)) for p, w in zip(pairs, w12)]
        mt = [(g[:PAIR] + jnp.where(eye, gam_end[:, p * PAIR:(p + 1) * PAIR], 0.0)).astype(BF16)
              for p, g in zip(pairs, mn)]
        nt = [g[PAIR:] + _tn(vb, bd(kh, p)) for p, g, vb in zip(pairs, mn, bd_v)]

        for p in pairs:
            s = state_ref[p]
            sb = s.astype(BF16)
            y_bd = _nt(q_hat[p], sb) + y0[p]
            y_ref[rows, p * PAIR:(p + 1) * PAIR] = (y_bd[:c_len] + y_bd[c_len:]).astype(y_ref.dtype)
            state_ref[p] = _dot(sb, mt[p]) + nt[p]
        return carry

    lax.fori_loop(0, n_chunks, chunk_body, 0)


def _rwkv_scan(r, ld, k, v, a, b, batch, seq, tc=512):
    steps = seq // tc
    n = r.shape[0]
    blk = pl.BlockSpec((tc, RW_DIM), lambda bb, t: (bb * steps + t, 0))
    ii = jnp.arange(CHUNK)
    tril = (ii[:, None] >= ii[None, :]).astype(BF16)
    return pl.pallas_call(
        functools.partial(_rwkv_scan_kernel, tc // CHUNK),
        grid=(batch, steps),
        in_specs=[blk] * 6 + [_const_spec((CHUNK, CHUNK))],
        out_specs=blk,
        out_shape=jax.ShapeDtypeStruct((n, RW_DIM), BF16),
        scratch_shapes=[pltpu.VMEM((N_PAIRS, PAIR, PAIR), F32)],
        compiler_params=_cparams(("parallel", "arbitrary")),
        name="rwkv_scan",
    )(r, ld, k, v, a, b, tril)


def _merge_kernel(x_ref, y_ref, g_ref, bonus_ref, yatt_ref, gates_ref, gnw_ref, gnb_ref, ones_ref,
                  worw_ref, wout_ref, npost_ref, o_ref):
    ones_bd = ones_ref[...]
    y = y_ref[...].astype(F32)
    mean = _head_sum(y, ones_bd) * (1.0 / RW_HEAD)
    d = y - mean
    var = _head_sum(d * d, ones_bd) * (1.0 / RW_HEAD)
    yn = d * lax.rsqrt(var + GN_EPS) * gnw_ref[...] + gnb_ref[...]
    rw_out = ((yn + bonus_ref[...].astype(F32)) * g_ref[...].astype(F32)).astype(BF16)
    y_rw = _dot(rw_out, worw_ref[...])
    gates = gates_ref[...].astype(F32)
    mixed = gates[:, :D_MODEL] * yatt_ref[...].astype(F32) + gates[:, D_MODEL:] * y_rw
    out = _dot(mixed.astype(BF16), wout_ref[...])
    o_ref[...] = x_ref[...] + _rmsnorm(out, npost_ref[...])


def _merge(x, y, g, bonus, yatt, gates, gn_w, gn_b, ones_bd, w_o_rw, w_out, norm_post, tm=256):
    n = x.shape[0]
    row = lambda w: pl.BlockSpec((tm, w), lambda i: (i, 0))
    vec = _const_spec((1, D_MODEL))
    mat = _const_spec((D_MODEL, D_MODEL))
    return pl.pallas_call(
        _merge_kernel,
        grid=(n // tm,),
        in_specs=[row(D_MODEL)] * 5 + [row(2 * D_MODEL), vec, vec, _const_spec(ones_bd.shape), mat, mat, vec],
        out_specs=row(D_MODEL),
        out_shape=jax.ShapeDtypeStruct((n, D_MODEL), F32),
        compiler_params=_cparams(("parallel",)),
        name="mixer_merge",
    )(x, y, g, bonus, yatt, gates, gn_w, gn_b, ones_bd, w_o_rw, w_out, norm_post)


def _gelu_tanh(x):
    return 0.5 * x * (1.0 + jnp.tanh(0.7978845608028654 * (x + 0.044715 * (x * x * x))))


def _ffn_kernel(tiles_per_seq, tm, x_ref, p_ref, npre_ref, wup_ref, cw_ref, cb_ref, wdown_ref, npost_ref,
                wple_ref, wpg_ref, nple_ref, o_ref, ubuf_ref):
    first = (pl.program_id(0) % tiles_per_seq) == 0

    @pl.when(first)
    def _():
        ubuf_ref[0:8, :] = jnp.zeros((8, 2 * FFN_DIM), F32)

    @pl.when(jnp.logical_not(first))
    def _():
        ubuf_ref[0:8, :] = ubuf_ref[tm:tm + 8, :]

    x = x_ref[...]
    h = _rmsnorm(x, npre_ref[...]).astype(BF16)
    ubuf_ref[8:tm + 8, :] = _dot(h, wup_ref[...])
    cw = cw_ref[...]
    u = (ubuf_ref[6:tm + 6, :] * cw[0:1, :] + ubuf_ref[7:tm + 7, :] * cw[1:2, :]
         + ubuf_ref[8:tm + 8, :] * cw[2:3, :] + cb_ref[...])
    act = (_gelu_tanh(u[:, :FFN_DIM]) * u[:, FFN_DIM:]).astype(BF16)
    x = x + _rmsnorm(_dot(act, wdown_ref[...]), npost_ref[...])
    e = _dot(p_ref[...].astype(BF16), wple_ref[...])
    gate = _sigmoid(_dot(x.astype(BF16), wpg_ref[...]))
    o_ref[...] = x + _rmsnorm(gate * e, nple_ref[...])


def _ffn_ple(x, p, norm_pre, w_up, conv_w8, conv_b, w_down, norm_post, w_ple, w_pg, norm_ple, seq, tm=256):
    n = x.shape[0]
    row = lambda w: pl.BlockSpec((tm, w), lambda i: (i, 0))
    vec = _const_spec((1, D_MODEL))
    return pl.pallas_call(
        functools.partial(_ffn_kernel, seq // tm, tm),
        grid=(n // tm,),
        in_specs=[row(D_MODEL), row(PLE_DIM), vec, _const_spec(w_up.shape), _const_spec(conv_w8.shape),
                  _const_spec((1, 2 * FFN_DIM)), _const_spec(w_down.shape), vec, _const_spec(w_ple.shape),
                  _const_spec(w_pg.shape), vec],
        out_specs=row(D_MODEL),
        out_shape=jax.ShapeDtypeStruct((n, D_MODEL), F32),
        scratch_shapes=[pltpu.VMEM((tm + 8, 2 * FFN_DIM), F32)],
        compiler_params=_cparams(("arbitrary",)),
        name="ffn_ple",
    )(x, p, norm_pre, w_up, conv_w8, conv_b, w_down, norm_post, w_ple, w_pg, norm_ple)


def _layer(x, p_i, cos, sin, ones_bd, batch, seq, norm_mix_pre, w_in, att_sinks, w_o_att, rw_mu, rw_w0, rw_w2,
           rw_a0, rw_a2, rw_g2, rw_k_k, rw_k_a, rw_r_k, rw_gn_w, rw_gn_b, w_o_rw, w_out, norm_mix_post,
           norm_ffn_pre, w_up, conv_w, conv_b, w_down, norm_ffn_post, w_ple, w_ple_gate, norm_ple):
    vec = lambda t: t.reshape(1, -1)
    o_rw = ATT_Q + 2 * ATT_KV
    o_lora = o_rw + 3 * RW_DIM
    o_gate = o_rw + RW_SHIFT_COLS
    per_member = lambda w: jnp.broadcast_to(
        w.reshape(D_MODEL, N_KV_HEADS, 1, HEAD_DIM), (D_MODEL, N_KV_HEADS, Q_PER_KV, HEAD_DIM)).reshape(D_MODEL, ATT_Q)
    wqkv = jnp.concatenate([w_in[:, :ATT_Q], per_member(w_in[:, ATT_Q:ATT_Q + ATT_KV]),
                            per_member(w_in[:, ATT_Q + ATT_KV:o_rw])], axis=1).astype(BF16)
    wrkv = w_in[:, o_rw:o_lora].astype(BF16)
    wlora = jnp.pad(w_in[:, o_lora:o_gate], ((0, 0), (0, LORA_PAD - LORA_DIM))).astype(BF16)
    wgate = w_in[:, o_gate:].astype(BF16)
    q, k, v, zrkv, zlora, gates = _inproj(x, vec(norm_mix_pre), cos, sin, wqkv, wrkv, wlora, wgate)

    sinks_tiled = jnp.broadcast_to(att_sinks.reshape(N_Q_HEADS, 1), (N_Q_HEADS, LANES))
    y_att = _attention(q, k, v, sinks_tiled, w_o_att.astype(BF16), batch, seq)

    mu_rkv = vec(rw_mu[:3 * RW_DIM])
    mu_lora = vec(jnp.pad(rw_mu[3 * RW_DIM:], (0, LORA_PAD - LORA_DIM)))
    wl = jnp.zeros((LORA_PAD, 3 * RW_DIM), F32)
    wl = wl.at[:DECAY_RANK, :RW_DIM].set(rw_w2)
    wl = wl.at[DECAY_RANK:DECAY_RANK + ICLR_RANK, RW_DIM:2 * RW_DIM].set(rw_a2)
    wl = wl.at[DECAY_RANK + ICLR_RANK:LORA_DIM, 2 * RW_DIM:].set(rw_g2)
    r, ld, k2, v2, a, b, g, bonus = _rwkv_prep(
        zrkv, zlora, mu_rkv, mu_lora, wl.astype(BF16), vec(rw_w0), vec(rw_a0), vec(rw_k_k), vec(rw_k_a),
        vec(rw_r_k), ones_bd, seq)
    y = _rwkv_scan(r, ld, k2, v2, a, b, batch, seq)

    x = _merge(x, y, g, bonus, y_att, gates, vec(rw_gn_w), vec(rw_gn_b), ones_bd, w_o_rw.astype(BF16),
               w_out.astype(BF16), vec(norm_mix_post))

    conv_w8 = jnp.pad(conv_w, ((0, 8 - CONV_W), (0, 0)))
    return _ffn_ple(x, p_i, vec(norm_ffn_pre), w_up.astype(BF16), conv_w8, vec(conv_b), w_down.astype(BF16),
                    vec(norm_ffn_post), w_ple.astype(BF16), w_ple_gate.astype(BF16), vec(norm_ple), seq)


def kernel(x, p, positions, norm_mix_pre, w_in, att_sinks, w_o_att, rw_mu, rw_w0, rw_w2, rw_a0, rw_a2, rw_g2, rw_k_k, rw_k_a, rw_r_k, rw_gn_w, rw_gn_b, w_o_rw, w_out, norm_mix_post, norm_ffn_pre, w_up, conv_w, conv_b, w_down, norm_ffn_post, w_ple, w_ple_gate, norm_ple):
    batch, seq, _ = x.shape
    depth = p.shape[0]
    n = batch * seq
    assert seq % 512 == 0 and x.shape[2] == D_MODEL
    cos, sin = _rope_table(positions.reshape(n, 1).astype(jnp.int32))
    hid = jnp.arange(4 * RW_HEAD) // RW_HEAD
    ones_bd = (hid[:, None] == hid[None, :]).astype(BF16)
    xf = x.reshape(n, D_MODEL)
    per_layer = (norm_mix_pre, w_in, att_sinks, w_o_att, rw_mu, rw_w0, rw_w2, rw_a0, rw_a2, rw_g2, rw_k_k,
                 rw_k_a, rw_r_k, rw_gn_w, rw_gn_b, w_o_rw, w_out, norm_mix_post, norm_ffn_pre, w_up, conv_w,
                 conv_b, w_down, norm_ffn_post, w_ple, w_ple_gate, norm_ple)
    for i in range(depth):
        xf = _layer(xf, p[i].reshape(n, PLE_DIM), cos, sin, ones_bd, batch, seq, *(t[i] for t in per_layer))
    return xf.reshape(batch, seq, D_MODEL)
```

```python
import functools

import jax
import jax.numpy as jnp
from jax import lax
from jax.experimental import pallas as pl
from jax.experimental.pallas import tpu as pltpu

F32 = jnp.float32
BF16 = jnp.bfloat16

D_MODEL = 1024
PLE_DIM = 256
HEAD_DIM = 64
N_Q_HEADS = 16
N_KV_HEADS = 4
Q_PER_KV = N_Q_HEADS // N_KV_HEADS
WINDOW = 128
BLOCK = 128
ROPE_THETA = 10000.0
ATT_Q = N_Q_HEADS * HEAD_DIM
ATT_KV = N_KV_HEADS * HEAD_DIM
RW_HEADS = 16
RW_HEAD = 64
RW_DIM = RW_HEADS * RW_HEAD
DECAY_RANK = 64
ICLR_RANK = 64
GATE_RANK = 160
LORA_DIM = DECAY_RANK + ICLR_RANK + GATE_RANK
LORA_PAD = 384
RW_SHIFT_COLS = 3 * RW_DIM + LORA_DIM
FFN_DIM = 2816
CONV_W = 3
NORM_EPS = 1e-6
GN_EPS = 64e-5

LANES = 128
CHUNK = 64
VMEM_LIMIT = 56 * 1024 * 1024


def _cparams(sem):
    return pltpu.CompilerParams(dimension_semantics=sem, vmem_limit_bytes=VMEM_LIMIT)


def _const_spec(shape):
    nd = len(shape)
    return pl.BlockSpec(shape, lambda *_: (0,) * nd, pipeline_mode=pl.Buffered(1))


def _rmsnorm(x, g):
    return x * lax.rsqrt(jnp.mean(x * x, axis=-1, keepdims=True) + NORM_EPS) * g


def _sigmoid(x):
    return 1.0 / (1.0 + jnp.exp(-x))


def _dot(a, b):
    return jnp.dot(a, b, preferred_element_type=F32)


def _split3(x):
    x1 = x.astype(BF16)
    r1 = x - x1.astype(F32)
    x2 = r1.astype(BF16)
    x3 = (r1 - x2.astype(F32)).astype(BF16)
    return x1, x2, x3


def _head_sum(x, ones_bd):
    x1 = x.astype(BF16)
    x2 = (x - x1.astype(F32)).astype(BF16)
    w = ones_bd.shape[0]
    outs = []
    for i in range(x.shape[1] // w):
        sl = slice(i * w, (i + 1) * w)
        outs.append(_dot(x1[:, sl], ones_bd) + _dot(x2[:, sl], ones_bd))
    return jnp.concatenate(outs, axis=1)


def _rope_table_kernel(pos_ref, cos_ref, sin_ref):
    half = HEAD_DIM // 2
    lane = lax.broadcasted_iota(jnp.int32, (1, LANES), 1)
    fidx = (lane & (half - 1)).astype(F32)
    inv_freq = jnp.power(ROPE_THETA, -fidx / half)
    ang = pos_ref[...].astype(F32) * inv_freq
    sign = jnp.where((lane & half) == 0, -1.0, 1.0)
    cos_ref[...] = jnp.cos(ang)
    sin_ref[...] = jnp.sin(ang) * sign


def _rope_table(pos, tm=512):
    n = pos.shape[0]
    return pl.pallas_call(
        _rope_table_kernel,
        grid=(n // tm,),
        in_specs=[pl.BlockSpec((tm, 1), lambda i: (i, 0))],
        out_specs=[pl.BlockSpec((tm, LANES), lambda i: (i, 0))] * 2,
        out_shape=[jax.ShapeDtypeStruct((n, LANES), F32)] * 2,
        compiler_params=_cparams(("parallel",)),
        name="rope_table",
    )(pos)


def _rope(t, cos, sin):
    w = t.shape[1]
    reps = w // LANES
    half = HEAD_DIM // 2
    cos_w = jnp.concatenate([cos] * reps, axis=1)
    sin_w = jnp.concatenate([sin] * reps, axis=1)
    lane = lax.broadcasted_iota(jnp.int32, t.shape, 1)
    swapped = jnp.where((lane & half) == 0, pltpu.roll(t, w - half, 1), pltpu.roll(t, half, 1))
    return t * cos_w + swapped * sin_w


def _shift_rows(z, prev_rows):
    rolled = pltpu.roll(z, 1, 0)
    row = lax.broadcasted_iota(jnp.int32, z.shape, 0)
    return jnp.where(row == 0, prev_rows[prev_rows.shape[0] - 1:, :], rolled)


def _mixer_in_kernel(tiles_per_seq, x_ref, g_ref, cos_ref, sin_ref, wqkv_ref, rep_ref, wgate_ref, wrkv_ref,
                     wlora_ref, mu_rkv_ref, mu_lora_ref, wl_ref, w0_ref, a0_ref, kk_ref, ka_ref, rk_ref, ones_ref,
                     q_ref, k_ref, v_ref, gates_ref, r_ref, ld_ref, k2_ref, v2_ref, a_ref, b_ref, g2_ref, bonus_ref,
                     zrkv_tail_ref, zlora_tail_ref):
    @pl.when((pl.program_id(0) % tiles_per_seq) == 0)
    def _():
        zrkv_tail_ref[...] = jnp.zeros_like(zrkv_tail_ref)
        zlora_tail_ref[...] = jnp.zeros_like(zlora_tail_ref)

    tm = x_ref.shape[0]
    h = _rmsnorm(x_ref[...], g_ref[...]).astype(BF16)
    cos = cos_ref[...]
    sin = sin_ref[...]

    qkv = _dot(h, wqkv_ref[...])
    q_ref[...] = (_rope(qkv[:, :ATT_Q], cos, sin) * (HEAD_DIM ** -0.5)).astype(BF16)
    rep = rep_ref[...]
    k_ref[...] = _dot(_rope(qkv[:, ATT_Q:ATT_Q + ATT_KV], cos, sin).astype(BF16), rep).astype(BF16)
    v_ref[...] = _dot(qkv[:, ATT_Q + ATT_KV:].astype(BF16), rep).astype(BF16)
    gates_ref[...] = _sigmoid(_dot(h, wgate_ref[...])).astype(BF16)

    z = _dot(h, wrkv_ref[...])
    zl = _dot(h, wlora_ref[...])
    zs = z + (_shift_rows(z, zrkv_tail_ref[...]) - z) * mu_rkv_ref[...]
    zls = zl + (_shift_rows(zl, zlora_tail_ref[...]) - zl) * mu_lora_ref[...]
    zrkv_tail_ref[...] = z[tm - 8:, :]
    zlora_tail_ref[...] = zl[tm - 8:, :]

    lane = lax.broadcasted_iota(jnp.int32, zls.shape, 1)
    act = jnp.where(lane < DECAY_RANK, jnp.tanh(zls),
                    jnp.where(lane < DECAY_RANK + ICLR_RANK, zls, _sigmoid(zls)))
    lo = _dot(act.astype(BF16), wl_ref[...])
    r = zs[:, :RW_DIM]
    k = zs[:, RW_DIM:2 * RW_DIM]
    v = zs[:, 2 * RW_DIM:]
    wpre = -(w0_ref[...] + lo[:, :RW_DIM])
    softplus = jnp.maximum(wpre, 0.0) + jnp.log(1.0 + jnp.exp(-jnp.abs(wpre)))
    w = -softplus - 0.5
    iclr = _sigmoid(a0_ref[...] + lo[:, RW_DIM:2 * RW_DIM])
    ones_bd = ones_ref[...]
    kkr = k * kk_ref[...]
    kk = kkr / jnp.maximum(jnp.sqrt(_head_sum(kkr * kkr, ones_bd)), 1e-12)
    k2 = k * (1.0 + (iclr - 1.0) * ka_ref[...])
    r_ref[...] = r.astype(BF16)
    ld_ref[...] = -jnp.exp(w)
    k2_ref[...] = k2.astype(BF16)
    v2_ref[...] = v.astype(BF16)
    a_ref[...] = (-kk).astype(BF16)
    b_ref[...] = (kk * iclr).astype(BF16)
    g2_ref[...] = lo[:, 2 * RW_DIM:].astype(BF16)
    bonus_ref[...] = (_head_sum(r * k2 * rk_ref[...], ones_bd) * v).astype(BF16)


def _mixer_in(x, g, cos, sin, wqkv, rep, wgate, wrkv, wlora, mu_rkv, mu_lora, wl, w0, a0, k_k, k_a, r_k, ones_bd,
              seq, tm=256):
    n = x.shape[0]
    row = lambda w: pl.BlockSpec((tm, w), lambda i: (i, 0))
    vec = lambda w: _const_spec((1, w))
    consts = [wqkv, rep, wgate, wrkv, wlora]
    out_dtypes = [BF16] * 5 + [F32] + [BF16] * 6
    out_widths = [ATT_Q, ATT_Q, ATT_Q, 2 * D_MODEL] + [RW_DIM] * 8
    return pl.pallas_call(
        functools.partial(_mixer_in_kernel, seq // tm),
        grid=(n // tm,),
        in_specs=[row(D_MODEL), vec(D_MODEL), row(LANES), row(LANES)] + [_const_spec(c.shape) for c in consts]
                 + [vec(3 * RW_DIM), vec(LORA_PAD), _const_spec(wl.shape)] + [vec(RW_DIM)] * 5
                 + [_const_spec(ones_bd.shape)],
        out_specs=[row(w) for w in out_widths],
        out_shape=[jax.ShapeDtypeStruct((n, w), dt) for w, dt in zip(out_widths, out_dtypes)],
        scratch_shapes=[pltpu.VMEM((8, 3 * RW_DIM), F32), pltpu.VMEM((8, LORA_PAD), F32)],
        compiler_params=_cparams(("arbitrary",)),
        name="mixer_in",
    )(x, g, cos, sin, wqkv, rep, wgate, wrkv, wlora, mu_rkv, mu_lora, wl, w0, a0, k_k, k_a, r_k, ones_bd)


GROUP_W = Q_PER_KV * HEAD_DIM
KEYS = 2 * BLOCK


def _attn_kernel(q_ref, kc_ref, kp_ref, vc_ref, vp_ref, bias_ref, bdmask_ref, sink_ref, wo_ref, o_ref):
    q = q_ref[...]
    kcat = jnp.concatenate([kp_ref[...], kc_ref[...]], axis=0)
    vcat = jnp.concatenate([vp_ref[...], vc_ref[...]], axis=0)
    bias = bias_ref[...]
    bdmask = bdmask_ref[...]
    lane = lax.broadcasted_iota(jnp.int32, (BLOCK, GROUP_W), 1)
    kv_heads = range(N_KV_HEADS)
    groups = [slice(j * GROUP_W, (j + 1) * GROUP_W) for j in kv_heads]

    kbd = [jnp.concatenate([kcat[:, sl]] * Q_PER_KV, axis=0) * bdmask for sl in groups]
    s = [_nt(q[:, sl], kb) + bias for sl, kb in zip(groups, kbd)]
    p, sink_terms = [], []
    for j in kv_heads:
        pj, ej = [], []
        for g in range(Q_PER_KV):
            sg = s[j][:, g * KEYS:(g + 1) * KEYS]
            h = j * Q_PER_KV + g
            sink = sink_ref[h:h + 1, 0:1]
            m = jnp.maximum(jnp.max(sg, axis=-1, keepdims=True), sink)
            pj.append(jnp.exp(sg - m).astype(BF16))
            ej.append(jnp.exp(sink - m))
        p.append(jnp.concatenate(pj, axis=1))
        sink_terms.append(ej)
    vbd = [jnp.concatenate([jnp.concatenate([vcat[:, sl]] * Q_PER_KV, axis=0) * bdmask, bdmask], axis=1)
           for sl in groups]
    ov = [_dot(pj, vb) for pj, vb in zip(p, vbd)]
    outs = []
    for j in kv_heads:
        e = sink_terms[j]
        sink_add = e[Q_PER_KV - 1]
        for g in reversed(range(Q_PER_KV - 1)):
            sink_add = jnp.where(lane < (g + 1) * HEAD_DIM, e[g], sink_add)
        outs.append(ov[j][:, :GROUP_W] / (ov[j][:, GROUP_W:] + sink_add))
    o = jnp.concatenate(outs, axis=1).astype(BF16)
    o_ref[...] = _dot(o, wo_ref[...]).astype(o_ref.dtype)


def _attention(q, k, v, sinks_tiled, wo, batch, seq):
    nb = seq // BLOCK
    cur = lambda b, n: (b * nb + n, 0)
    prev = lambda b, n: (b * nb + jnp.maximum(n - 1, 0), 0)
    qi = jnp.arange(BLOCK)[:, None]
    kj = jnp.arange(KEYS)[None, :]
    dist = qi + BLOCK - kj
    band = (dist >= 0) & (dist < WINDOW)
    masks = jnp.stack([band & (kj >= BLOCK), band])
    bias = jnp.tile(jnp.where(masks, 0.0, -jnp.inf).astype(F32), (1, 1, Q_PER_KV))
    member_r = jnp.arange(Q_PER_KV * KEYS)[:, None] // KEYS
    member_c = jnp.arange(GROUP_W)[None, :] // HEAD_DIM
    bdmask = (member_r == member_c).astype(BF16)
    blk = lambda imap: pl.BlockSpec((BLOCK, ATT_Q), imap)
    return pl.pallas_call(
        _attn_kernel,
        grid=(batch, nb),
        in_specs=[blk(cur), blk(cur), blk(prev), blk(cur), blk(prev),
                  pl.BlockSpec((None, BLOCK, Q_PER_KV * KEYS), lambda b, n: (jnp.minimum(n, 1), 0, 0)),
                  _const_spec(bdmask.shape), _const_spec(sinks_tiled.shape), _const_spec(wo.shape)],
        out_specs=pl.BlockSpec((BLOCK, D_MODEL), cur),
        out_shape=jax.ShapeDtypeStruct((batch * seq, D_MODEL), BF16),
        compiler_params=_cparams(("parallel", "parallel")),
        name="swa_attention",
    )(q, k, k, v, v, bias, bdmask, sinks_tiled, wo)


PAIR = 2 * RW_HEAD
N_PAIRS = RW_DIM // PAIR


def _nt(a, b):
    return lax.dot_general(a, b, (((1,), (1,)), ((), ())), preferred_element_type=F32)


def _tn(a, b):
    return lax.dot_general(a, b, (((0,), (0,)), ((), ())), preferred_element_type=F32)


def _rwkv_scan_kernel(n_chunks, r_ref, ld_ref, k_ref, v_ref, a_ref, b_ref, tril_ref, y_ref, state_ref):
    @pl.when(pl.program_id(1) == 0)
    def _():
        state_ref[...] = jnp.zeros_like(state_ref)

    c_len = CHUNK
    row = lax.broadcasted_iota(jnp.int32, (PAIR, PAIR), 0)
    col = lax.broadcasted_iota(jnp.int32, (PAIR, PAIR), 1)
    same_head = (row // c_len) == (col // c_len)
    strict = same_head & ((row % c_len) > (col % c_len))
    incl = same_head & ((row % c_len) >= (col % c_len))
    eye = row == col
    eye_f = eye.astype(F32)
    first_head = (lax.broadcasted_iota(jnp.int32, (c_len, RW_DIM), 1) & RW_HEAD) == 0
    tril = tril_ref[...]
    pairs = range(N_PAIRS)

    def halves(x):
        return jnp.where(first_head, x, 0.0).astype(BF16), jnp.where(first_head, 0.0, x).astype(BF16)

    def bd(hv, p):
        sl = slice(p * PAIR, (p + 1) * PAIR)
        return jnp.concatenate([hv[0][:, sl], hv[1][:, sl]], axis=0)

    def chunk_body(c, carry):
        rows = pl.ds(pl.multiple_of(c * c_len, c_len), c_len)
        ld = ld_ref[rows, :]
        l1, l2, l3 = _split3(ld)
        cum = _dot(tril, l1) + (_dot(tril, l2) + _dot(tril, l3))
        end = cum[c_len - 1:c_len, :]
        g_inv = jnp.exp(-cum)
        g_end = jnp.exp(end - cum)
        gam_end = jnp.exp(end)
        a = a_ref[rows, :].astype(F32)
        b = b_ref[rows, :].astype(F32)
        k = k_ref[rows, :].astype(F32)
        at = halves(a * jnp.exp(cum - ld))
        rt = halves(r_ref[rows, :].astype(F32) * jnp.exp(cum))
        bt = halves(b * g_inv)
        kt = halves(k * g_inv)
        bh = halves(b * g_end)
        kh = halves(k * g_end)
        vv = halves(v_ref[rows, :].astype(F32))
        bd_at = [bd(at, p) for p in pairs]
        bd_rt = [bd(rt, p) for p in pairs]
        bd_v = [bd(vv, p) for p in pairs]

        g1 = [_nt(jnp.concatenate([bd_at[p], bd_rt[p]], axis=0),
                  jnp.concatenate([bd(bt, p), bd(kt, p)], axis=0)) for p in pairs]
        a_ab = [jnp.where(strict, g[:PAIR, :PAIR], 0.0) for g in g1]
        a_ak = [jnp.where(strict, g[:PAIR, PAIR:], 0.0).astype(BF16) for g in g1]
        a_rb = [jnp.where(incl, g[PAIR:, :PAIR], 0.0).astype(BF16) for g in g1]
        a_rk = [jnp.where(incl, g[PAIR:, PAIR:], 0.0).astype(BF16) for g in g1]

        x = [eye_f + m for m in a_ab]
        pw = [m.astype(BF16) for m in a_ab]
        pw = [_dot(m, m).astype(BF16) for m in pw]
        for _ in range(c_len.bit_length() - 3):
            res = [_dot(jnp.concatenate([xm.astype(BF16), pm], axis=0), pm) for xm, pm in zip(x, pw)]
            x = [xm + rs[:PAIR] for xm, rs in zip(x, res)]
            pw = [rs[PAIR:].astype(BF16) for rs in res]
        t_inv = [(xm + _dot(xm.astype(BF16), pm)).astype(BF16) for xm, pm in zip(x, pw)]

        av = [_dot(m, vb).astype(BF16) for m, vb in zip(a_ak, bd_v)]
        w12 = [_dot(t, jnp.concatenate([ab, avp], axis=1)).astype(BF16) for t, ab, avp in zip(t_inv, bd_at, av)]
        qy = [_dot(m, w) for m, w in zip(a_rb, w12)]
        q_hat = [(rb.astype(F32) + q[:, :PAIR]).astype(BF16) for rb, q in zip(bd_rt, qy)]
        y0 = [q[:, PAIR:] + _dot(m, vb) for q, m, vb in zip(qy, a_rk, bd_v)]
        mn = [_tn(w, bd(bh, p)) for p, w in zip(pairs, w12)]
        mt = [(g[:PAIR] + jnp.where(eye, gam_end[:, p * PAIR:(p + 1) * PAIR], 0.0)).astype(BF16)
              for p, g in zip(pairs, mn)]
        nt = [g[PAIR:] + _tn(vb, bd(kh, p)) for p, g, vb in zip(pairs, mn, bd_v)]

        for p in pairs:
            s = state_ref[p]
            sb = s.astype(BF16)
            y_bd = _nt(q_hat[p], sb) + y0[p]
            y_ref[rows, p * PAIR:(p + 1) * PAIR] = (y_bd[:c_len] + y_bd[c_len:]).astype(y_ref.dtype)
            state_ref[p] = _dot(sb, mt[p]) + nt[p]
        return carry

    lax.fori_loop(0, n_chunks, chunk_body, 0)


def _rwkv_scan(r, ld, k, v, a, b, batch, seq, tc=512):
    steps = seq // tc
    n = r.shape[0]
    blk = pl.BlockSpec((tc, RW_DIM), lambda bb, t: (bb * steps + t, 0))
    ii = jnp.arange(CHUNK)
    tril = (ii[:, None] >= ii[None, :]).astype(BF16)
    return pl.pallas_call(
        functools.partial(_rwkv_scan_kernel, tc // CHUNK),
        grid=(batch, steps),
        in_specs=[blk] * 6 + [_const_spec((CHUNK, CHUNK))],
        out_specs=blk,
        out_shape=jax.ShapeDtypeStruct((n, RW_DIM), BF16),
        scratch_shapes=[pltpu.VMEM((N_PAIRS, PAIR, PAIR), F32)],
        compiler_params=_cparams(("parallel", "arbitrary")),
        name="rwkv_scan",
    )(r, ld, k, v, a, b, tril)


def _merge_kernel(x_ref, y_ref, g_ref, bonus_ref, yatt_ref, gates_ref, gnw_ref, gnb_ref, ones_ref,
                  worw_ref, wout_ref, npost_ref, o_ref):
    ones_bd = ones_ref[...]
    y = y_ref[...].astype(F32)
    mean = _head_sum(y, ones_bd) * (1.0 / RW_HEAD)
    d = y - mean
    var = _head_sum(d * d, ones_bd) * (1.0 / RW_HEAD)
    yn = d * lax.rsqrt(var + GN_EPS) * gnw_ref[...] + gnb_ref[...]
    rw_out = ((yn + bonus_ref[...].astype(F32)) * g_ref[...].astype(F32)).astype(BF16)
    y_rw = _dot(rw_out, worw_ref[...])
    gates = gates_ref[...].astype(F32)
    mixed = gates[:, :D_MODEL] * yatt_ref[...].astype(F32) + gates[:, D_MODEL:] * y_rw
    out = _dot(mixed.astype(BF16), wout_ref[...])
    o_ref[...] = x_ref[...] + _rmsnorm(out, npost_ref[...])


def _merge(x, y, g, bonus, yatt, gates, gn_w, gn_b, ones_bd, w_o_rw, w_out, norm_post, tm=256):
    n = x.shape[0]
    row = lambda w: pl.BlockSpec((tm, w), lambda i: (i, 0))
    vec = _const_spec((1, D_MODEL))
    mat = _const_spec((D_MODEL, D_MODEL))
    return pl.pallas_call(
        _merge_kernel,
        grid=(n // tm,),
        in_specs=[row(D_MODEL)] * 5 + [row(2 * D_MODEL), vec, vec, _const_spec(ones_bd.shape), mat, mat, vec],
        out_specs=row(D_MODEL),
        out_shape=jax.ShapeDtypeStruct((n, D_MODEL), F32),
        compiler_params=_cparams(("parallel",)),
        name="mixer_merge",
    )(x, y, g, bonus, yatt, gates, gn_w, gn_b, ones_bd, w_o_rw, w_out, norm_post)


def _gelu_tanh(x):
    return 0.5 * x * (1.0 + jnp.tanh(0.7978845608028654 * (x + 0.044715 * (x * x * x))))


def _ffn_kernel(tiles_per_seq, tm, x_ref, p_ref, npre_ref, wup_ref, cw_ref, cb_ref, wdown_ref, npost_ref,
                wple_ref, wpg_ref, nple_ref, o_ref, ubuf_ref):
    first = (pl.program_id(0) % tiles_per_seq) == 0

    @pl.when(first)
    def _():
        ubuf_ref[0:8, :] = jnp.zeros((8, 2 * FFN_DIM), F32)

    @pl.when(jnp.logical_not(first))
    def _():
        ubuf_ref[0:8, :] = ubuf_ref[tm:tm + 8, :]

    x = x_ref[...]
    h = _rmsnorm(x, npre_ref[...]).astype(BF16)
    ubuf_ref[8:tm + 8, :] = _dot(h, wup_ref[...])
    cw = cw_ref[...]
    u = (ubuf_ref[6:tm + 6, :] * cw[0:1, :] + ubuf_ref[7:tm + 7, :] * cw[1:2, :]
         + ubuf_ref[8:tm + 8, :] * cw[2:3, :] + cb_ref[...])
    act = (_gelu_tanh(u[:, :FFN_DIM]) * u[:, FFN_DIM:]).astype(BF16)
    x = x + _rmsnorm(_dot(act, wdown_ref[...]), npost_ref[...])
    e = _dot(p_ref[...].astype(BF16), wple_ref[...])
    gate = _sigmoid(_dot(x.astype(BF16), wpg_ref[...]))
    o_ref[...] = x + _rmsnorm(gate * e, nple_ref[...])


def _ffn_ple(x, p, norm_pre, w_up, conv_w8, conv_b, w_down, norm_post, w_ple, w_pg, norm_ple, seq, tm=256):
    n = x.shape[0]
    row = lambda w: pl.BlockSpec((tm, w), lambda i: (i, 0))
    vec = _const_spec((1, D_MODEL))
    return pl.pallas_call(
        functools.partial(_ffn_kernel, seq // tm, tm),
        grid=(n // tm,),
        in_specs=[row(D_MODEL), row(PLE_DIM), vec, _const_spec(w_up.shape), _const_spec(conv_w8.shape),
                  _const_spec((1, 2 * FFN_DIM)), _const_spec(w_down.shape), vec, _const_spec(w_ple.shape),
                  _const_spec(w_pg.shape), vec],
        out_specs=row(D_MODEL),
        out_shape=jax.ShapeDtypeStruct((n, D_MODEL), F32),
        scratch_shapes=[pltpu.VMEM((tm + 8, 2 * FFN_DIM), F32)],
        compiler_params=_cparams(("arbitrary",)),
        name="ffn_ple",
    )(x, p, norm_pre, w_up, conv_w8, conv_b, w_down, norm_post, w_ple, w_pg, norm_ple)


def _layer(x, p_i, cos, sin, ones_bd, batch, seq, norm_mix_pre, w_in, att_sinks, w_o_att, rw_mu, rw_w0, rw_w2,
           rw_a0, rw_a2, rw_g2, rw_k_k, rw_k_a, rw_r_k, rw_gn_w, rw_gn_b, w_o_rw, w_out, norm_mix_post,
           norm_ffn_pre, w_up, conv_w, conv_b, w_down, norm_ffn_post, w_ple, w_ple_gate, norm_ple):
    vec = lambda t: t.reshape(1, -1)
    o_rw = ATT_Q + 2 * ATT_KV
    o_lora = o_rw + 3 * RW_DIM
    o_gate = o_rw + RW_SHIFT_COLS
    wqkv = w_in[:, :o_rw].astype(BF16)
    wrkv = w_in[:, o_rw:o_lora].astype(BF16)
    wlora = jnp.pad(w_in[:, o_lora:o_gate], ((0, 0), (0, LORA_PAD - LORA_DIM))).astype(BF16)
    wgate = w_in[:, o_gate:].astype(BF16)
    src = jnp.arange(ATT_KV)[:, None]
    dst = jnp.arange(ATT_Q)[None, :]
    rep = ((src // HEAD_DIM == dst // GROUP_W) & (src % HEAD_DIM == dst % HEAD_DIM)).astype(BF16)
    mu_rkv = vec(rw_mu[:3 * RW_DIM])
    mu_lora = vec(jnp.pad(rw_mu[3 * RW_DIM:], (0, LORA_PAD - LORA_DIM)))
    wl = jnp.concatenate([
        jnp.pad(rw_w2, ((0, 0), (0, 2 * RW_DIM))),
        jnp.pad(rw_a2, ((0, 0), (RW_DIM, RW_DIM))),
        jnp.pad(rw_g2, ((0, LORA_PAD - LORA_DIM), (2 * RW_DIM, 0)))], axis=0).astype(BF16)
    q, k, v, gates, r, ld, k2, v2, a, b, g, bonus = _mixer_in(
        x, vec(norm_mix_pre), cos, sin, wqkv, rep, wgate, wrkv, wlora, mu_rkv, mu_lora, wl, vec(rw_w0),
        vec(rw_a0), vec(rw_k_k), vec(rw_k_a), vec(rw_r_k), ones_bd, seq)

    sinks_tiled = jnp.broadcast_to(att_sinks.reshape(N_Q_HEADS, 1), (N_Q_HEADS, LANES))
    y_att = _attention(q, k, v, sinks_tiled, w_o_att.astype(BF16), batch, seq)
    y = _rwkv_scan(r, ld, k2, v2, a, b, batch, seq)

    x = _merge(x, y, g, bonus, y_att, gates, vec(rw_gn_w), vec(rw_gn_b), ones_bd, w_o_rw.astype(BF16),
               w_out.astype(BF16), vec(norm_mix_post))

    conv_w8 = jnp.pad(conv_w, ((0, 8 - CONV_W), (0, 0)))
    return _ffn_ple(x, p_i, vec(norm_ffn_pre), w_up.astype(BF16), conv_w8, vec(conv_b), w_down.astype(BF16),
                    vec(norm_ffn_post), w_ple.astype(BF16), w_ple_gate.astype(BF16), vec(norm_ple), seq)


def kernel(x, p, positions, norm_mix_pre, w_in, att_sinks, w_o_att, rw_mu, rw_w0, rw_w2, rw_a0, rw_a2, rw_g2, rw_k_k, rw_k_a, rw_r_k, rw_gn_w, rw_gn_b, w_o_rw, w_out, norm_mix_post, norm_ffn_pre, w_up, conv_w, conv_b, w_down, norm_ffn_post, w_ple, w_ple_gate, norm_ple):
    batch, seq, _ = x.shape
    depth = p.shape[0]
    n = batch * seq
    assert seq % 512 == 0 and x.shape[2] == D_MODEL
    cos, sin = _rope_table(positions.reshape(n, 1).astype(jnp.int32))
    hid = jnp.arange(4 * RW_HEAD) // RW_HEAD
    ones_bd = (hid[:, None] == hid[None, :]).astype(BF16)
    xf = x.reshape(n, D_MODEL)
    per_layer = (norm_mix_pre, w_in, att_sinks, w_o_att, rw_mu, rw_w0, rw_w2, rw_a0, rw_a2, rw_g2, rw_k_k,
                 rw_k_a, rw_r_k, rw_gn_w, rw_gn_b, w_o_rw, w_out, norm_mix_post, norm_ffn_pre, w_up, conv_w,
                 conv_b, w_down, norm_ffn_post, w_ple, w_ple_gate, norm_ple)
    for i in range(depth):
        xf = _layer(xf, p[i].reshape(n, PLE_DIM), cos, sin, ones_bd, batch, seq, *(t[i] for t in per_layer))
    return xf.reshape(batch, seq, D_MODEL)
```

```python
import functools

import jax
import jax.numpy as jnp
from jax import lax
from jax.experimental import pallas as pl
from jax.experimental.pallas import tpu as pltpu

F32 = jnp.float32
BF16 = jnp.bfloat16

D_MODEL = 1024
PLE_DIM = 256
HEAD_DIM = 64
N_Q_HEADS = 16
N_KV_HEADS = 4
Q_PER_KV = N_Q_HEADS // N_KV_HEADS
WINDOW = 128
BLOCK = 128
ROPE_THETA = 10000.0
ATT_Q = N_Q_HEADS * HEAD_DIM
ATT_KV = N_KV_HEADS * HEAD_DIM
RW_HEADS = 16
RW_HEAD = 64
RW_DIM = RW_HEADS * RW_HEAD
DECAY_RANK = 64
ICLR_RANK = 64
GATE_RANK = 160
LORA_DIM = DECAY_RANK + ICLR_RANK + GATE_RANK
LORA_RANKS = (DECAY_RANK, ICLR_RANK, GATE_RANK)
LORA_SLOTS = (128, 128, 256)
LORA_PAD = sum(LORA_SLOTS)
RW_SHIFT_COLS = 3 * RW_DIM + LORA_DIM
FFN_DIM = 2816
CONV_W = 3
NORM_EPS = 1e-6
GN_EPS = 64e-5

LANES = 128
CHUNK = 64
VMEM_LIMIT = 56 * 1024 * 1024


def _cparams(sem):
    return pltpu.CompilerParams(dimension_semantics=sem, vmem_limit_bytes=VMEM_LIMIT)


def _const_spec(shape):
    nd = len(shape)
    return pl.BlockSpec(shape, lambda *_: (0,) * nd, pipeline_mode=pl.Buffered(1))


def _rmsnorm(x, g):
    return x * lax.rsqrt(jnp.mean(x * x, axis=-1, keepdims=True) + NORM_EPS) * g


def _sigmoid(x):
    return 1.0 / (1.0 + jnp.exp(-x))


def _dot(a, b):
    return jnp.dot(a, b, preferred_element_type=F32)


def _split3(x):
    x1 = x.astype(BF16)
    r1 = x - x1.astype(F32)
    x2 = r1.astype(BF16)
    x3 = (r1 - x2.astype(F32)).astype(BF16)
    return x1, x2, x3


def _head_sum(x, ones_bd):
    x1 = x.astype(BF16)
    x2 = (x - x1.astype(F32)).astype(BF16)
    w = ones_bd.shape[0]
    outs = []
    for i in range(x.shape[1] // w):
        sl = slice(i * w, (i + 1) * w)
        outs.append(_dot(x1[:, sl], ones_bd) + _dot(x2[:, sl], ones_bd))
    return jnp.concatenate(outs, axis=1)


def _rope_table_kernel(pos_ref, cos_ref, sin_ref):
    half = HEAD_DIM // 2
    lane = lax.broadcasted_iota(jnp.int32, (1, LANES), 1)
    fidx = (lane & (half - 1)).astype(F32)
    inv_freq = jnp.power(ROPE_THETA, -fidx / half)
    ang = pos_ref[...].astype(F32) * inv_freq
    sign = jnp.where((lane & half) == 0, -1.0, 1.0)
    cos_ref[...] = jnp.cos(ang)
    sin_ref[...] = jnp.sin(ang) * sign


def _rope_table(pos, tm=512):
    n = pos.shape[0]
    return pl.pallas_call(
        _rope_table_kernel,
        grid=(n // tm,),
        in_specs=[pl.BlockSpec((tm, 1), lambda i: (i, 0))],
        out_specs=[pl.BlockSpec((tm, LANES), lambda i: (i, 0))] * 2,
        out_shape=[jax.ShapeDtypeStruct((n, LANES), F32)] * 2,
        compiler_params=_cparams(("parallel",)),
        name="rope_table",
    )(pos)


def _rope(t, cos, sin):
    w = t.shape[1]
    reps = w // LANES
    half = HEAD_DIM // 2
    cos_w = jnp.concatenate([cos] * reps, axis=1)
    sin_w = jnp.concatenate([sin] * reps, axis=1)
    lane = lax.broadcasted_iota(jnp.int32, t.shape, 1)
    swapped = jnp.where((lane & half) == 0, pltpu.roll(t, w - half, 1), pltpu.roll(t, half, 1))
    return t * cos_w + swapped * sin_w


def _shift_rows(z, prev_rows):
    rolled = pltpu.roll(z, 1, 0)
    row = lax.broadcasted_iota(jnp.int32, z.shape, 0)
    return jnp.where(row == 0, prev_rows[prev_rows.shape[0] - 1:, :], rolled)


def _mixer_in_kernel(tiles_per_seq, x_ref, g_ref, cos_ref, sin_ref, wqkv_ref, rep_ref, wgate_ref, wrkv_ref,
                     wlora_ref, mu_rkv_ref, mu_lora_ref, wl_ref, w0_ref, a0_ref, kk_ref, ka_ref, rk_ref, ones_ref,
                     q_ref, k_ref, v_ref, gates_ref, r_ref, ld_ref, k2_ref, v2_ref, a_ref, b_ref, g2_ref, bonus_ref,
                     zrkv_tail_ref, zlora_tail_ref):
    @pl.when((pl.program_id(0) % tiles_per_seq) == 0)
    def _():
        zrkv_tail_ref[...] = jnp.zeros_like(zrkv_tail_ref)
        zlora_tail_ref[...] = jnp.zeros_like(zlora_tail_ref)

    tm = x_ref.shape[0]
    h = _rmsnorm(x_ref[...], g_ref[...]).astype(BF16)
    cos = cos_ref[...]
    sin = sin_ref[...]

    qkv = _dot(h, wqkv_ref[...])
    q_ref[...] = (_rope(qkv[:, :ATT_Q], cos, sin) * (HEAD_DIM ** -0.5)).astype(BF16)
    rep = rep_ref[...]
    k_ref[...] = _dot(_rope(qkv[:, ATT_Q:ATT_Q + ATT_KV], cos, sin).astype(BF16), rep).astype(BF16)
    v_ref[...] = _dot(qkv[:, ATT_Q + ATT_KV:].astype(BF16), rep).astype(BF16)
    gates_ref[...] = _sigmoid(_dot(h, wgate_ref[...])).astype(BF16)

    z = _dot(h, wrkv_ref[...])
    zl = _dot(h, wlora_ref[...])
    zs = z + (_shift_rows(z, zrkv_tail_ref[...]) - z) * mu_rkv_ref[...]
    zls = zl + (_shift_rows(zl, zlora_tail_ref[...]) - zl) * mu_lora_ref[...]
    zrkv_tail_ref[...] = z[tm - 8:, :]
    zlora_tail_ref[...] = zl[tm - 8:, :]

    s0, s1 = LORA_SLOTS[0], LORA_SLOTS[0] + LORA_SLOTS[1]
    lo_w = _dot(jnp.tanh(zls[:, :s0]).astype(BF16), wl_ref[:s0, :])
    lo_a = _dot(zls[:, s0:s1].astype(BF16), wl_ref[s0:s1, :])
    lo_g = _dot(_sigmoid(zls[:, s1:]).astype(BF16), wl_ref[s1:, :])
    r = zs[:, :RW_DIM]
    k = zs[:, RW_DIM:2 * RW_DIM]
    v = zs[:, 2 * RW_DIM:]
    wpre = -(w0_ref[...] + lo_w)
    softplus = jnp.maximum(wpre, 0.0) + jnp.log(1.0 + jnp.exp(-jnp.abs(wpre)))
    w = -softplus - 0.5
    iclr = _sigmoid(a0_ref[...] + lo_a)
    ones_bd = ones_ref[...]
    kkr = k * kk_ref[...]
    kk = kkr / jnp.maximum(jnp.sqrt(_head_sum(kkr * kkr, ones_bd)), 1e-12)
    k2 = k * (1.0 + (iclr - 1.0) * ka_ref[...])
    r_ref[...] = r.astype(BF16)
    ld_ref[...] = -jnp.exp(w)
    k2_ref[...] = k2.astype(BF16)
    v2_ref[...] = v.astype(BF16)
    a_ref[...] = (-kk).astype(BF16)
    b_ref[...] = (kk * iclr).astype(BF16)
    g2_ref[...] = lo_g.astype(BF16)
    bonus_ref[...] = (_head_sum(r * k2 * rk_ref[...], ones_bd) * v).astype(BF16)


def _mixer_in(x, g, cos, sin, wqkv, rep, wgate, wrkv, wlora, mu_rkv, mu_lora, wl, w0, a0, k_k, k_a, r_k, ones_bd,
              seq, tm=256):
    n = x.shape[0]
    row = lambda w: pl.BlockSpec((tm, w), lambda i: (i, 0))
    vec = lambda w: _const_spec((1, w))
    consts = [wqkv, rep, wgate, wrkv, wlora]
    out_dtypes = [BF16] * 5 + [F32] + [BF16] * 6
    out_widths = [ATT_Q, ATT_Q, ATT_Q, 2 * D_MODEL] + [RW_DIM] * 8
    return pl.pallas_call(
        functools.partial(_mixer_in_kernel, seq // tm),
        grid=(n // tm,),
        in_specs=[row(D_MODEL), vec(D_MODEL), row(LANES), row(LANES)] + [_const_spec(c.shape) for c in consts]
                 + [vec(3 * RW_DIM), vec(LORA_PAD), _const_spec(wl.shape)] + [vec(RW_DIM)] * 5
                 + [_const_spec(ones_bd.shape)],
        out_specs=[row(w) for w in out_widths],
        out_shape=[jax.ShapeDtypeStruct((n, w), dt) for w, dt in zip(out_widths, out_dtypes)],
        scratch_shapes=[pltpu.VMEM((8, 3 * RW_DIM), F32), pltpu.VMEM((8, LORA_PAD), F32)],
        compiler_params=_cparams(("arbitrary",)),
        name="mixer_in",
    )(x, g, cos, sin, wqkv, rep, wgate, wrkv, wlora, mu_rkv, mu_lora, wl, w0, a0, k_k, k_a, r_k, ones_bd)


GROUP_W = Q_PER_KV * HEAD_DIM
KEYS = 2 * BLOCK
Q_BLOCKS = 4


def _attn_kernel(q_ref, kc_ref, kp_ref, vc_ref, vp_ref, bias_ref, qmask_ref, sink_ref, wo_ref, o_ref):
    q = q_ref[...]
    kall = jnp.concatenate([kp_ref[...], kc_ref[...]], axis=0)
    vall = jnp.concatenate([vp_ref[...], vc_ref[...]], axis=0)
    bias_rest = bias_ref[1]
    bias_first = jnp.where(pl.program_id(1) == 0, bias_ref[0], bias_rest)
    qmask = qmask_ref[...]
    member_of_lane = lax.broadcasted_iota(jnp.int32, (BLOCK, GROUP_W), 1) // HEAD_DIM
    groups = [slice(j * GROUP_W, (j + 1) * GROUP_W) for j in range(N_KV_HEADS)]
    probs = [(blk, j) for blk in range(Q_BLOCKS) for j in range(N_KV_HEADS)]
    q_rows = lambda blk: slice(blk * BLOCK, (blk + 1) * BLOCK)
    k_rows = lambda blk: slice(blk * BLOCK, blk * BLOCK + KEYS)

    qm = [jnp.concatenate([q[q_rows(blk), groups[j]]] * Q_PER_KV, axis=0) * qmask for blk, j in probs]
    s = [_nt(qi, kall[k_rows(blk), groups[j]]) + (bias_first if blk == 0 else bias_rest)
         for qi, (blk, j) in zip(qm, probs)]
    sinks = [jnp.concatenate(
        [jnp.broadcast_to(sink_ref[h:h + 1, 0:1], (BLOCK, 1)) for h in range(j * Q_PER_KV, (j + 1) * Q_PER_KV)],
        axis=0) for _, j in probs]
    m = [jnp.maximum(jnp.max(si, axis=-1, keepdims=True), sk) for si, sk in zip(s, sinks)]
    p = [jnp.exp(si - mi).astype(BF16) for si, mi in zip(s, m)]
    ones_cols = jnp.ones((KEYS, LANES), BF16)
    ov = [_dot(pi, jnp.concatenate([vall[k_rows(blk), groups[j]], ones_cols], axis=1))
          for pi, (blk, j) in zip(p, probs)]
    outs = []
    for oi, sk, mi in zip(ov, sinks, m):
        rinv = 1.0 / (oi[:, GROUP_W:] + jnp.exp(sk - mi))
        on = oi[:, :GROUP_W] * jnp.concatenate([rinv] * (GROUP_W // LANES), axis=1)
        acc = on[(Q_PER_KV - 1) * BLOCK:]
        for g in reversed(range(Q_PER_KV - 1)):
            acc = jnp.where(member_of_lane == g, on[g * BLOCK:(g + 1) * BLOCK], acc)
        outs.append(acc)
    o = jnp.concatenate([jnp.concatenate(outs[blk * N_KV_HEADS:(blk + 1) * N_KV_HEADS], axis=1)
                         for blk in range(Q_BLOCKS)], axis=0).astype(BF16)
    o_ref[...] = _dot(o, wo_ref[...]).astype(o_ref.dtype)


def _attention(q, k, v, sinks_tiled, wo, batch, seq):
    nb = seq // BLOCK
    steps = nb // Q_BLOCKS
    cur = lambda b, n: (b * steps + n, 0)
    prev = lambda b, n: (b * nb + jnp.maximum(n * Q_BLOCKS - 1, 0), 0)
    qi = jnp.arange(BLOCK)[:, None]
    kj = jnp.arange(KEYS)[None, :]
    dist = qi + BLOCK - kj
    band = (dist >= 0) & (dist < WINDOW)
    masks = jnp.stack([band & (kj >= BLOCK), band])
    bias = jnp.tile(jnp.where(masks, 0.0, -jnp.inf).astype(F32), (1, Q_PER_KV, 1))
    member_r = jnp.arange(Q_PER_KV * BLOCK)[:, None] // BLOCK
    member_c = jnp.arange(GROUP_W)[None, :] // HEAD_DIM
    qmask = (member_r == member_c).astype(BF16)
    cur_blk = pl.BlockSpec((Q_BLOCKS * BLOCK, ATT_Q), cur)
    prev_blk = pl.BlockSpec((BLOCK, ATT_Q), prev)
    return pl.pallas_call(
        _attn_kernel,
        grid=(batch, steps),
        in_specs=[cur_blk, cur_blk, prev_blk, cur_blk, prev_blk, _const_spec(bias.shape),
                  _const_spec(qmask.shape), _const_spec(sinks_tiled.shape), _const_spec(wo.shape)],
        out_specs=pl.BlockSpec((Q_BLOCKS * BLOCK, D_MODEL), cur),
        out_shape=jax.ShapeDtypeStruct((batch * seq, D_MODEL), BF16),
        compiler_params=_cparams(("parallel", "parallel")),
        name="swa_attention",
    )(q, k, k, v, v, bias, qmask, sinks_tiled, wo)


PAIR = 2 * RW_HEAD
N_PAIRS = RW_DIM // PAIR


def _nt(a, b):
    return lax.dot_general(a, b, (((1,), (1,)), ((), ())), preferred_element_type=F32)


def _tn(a, b):
    return lax.dot_general(a, b, (((0,), (0,)), ((), ())), preferred_element_type=F32)


CHUNKS_PER_ITER = 2


def _rwkv_scan_kernel(n_iters, r_ref, ld_ref, k_ref, v_ref, a_ref, b_ref, tril_ref, y_ref, state_ref):
    @pl.when(pl.program_id(1) == 0)
    def _():
        state_ref[...] = jnp.zeros_like(state_ref)

    c_len = CHUNK
    row = lax.broadcasted_iota(jnp.int32, (PAIR, PAIR), 0)
    col = lax.broadcasted_iota(jnp.int32, (PAIR, PAIR), 1)
    same_head = (row // c_len) == (col // c_len)
    strict = same_head & ((row % c_len) > (col % c_len))
    incl = same_head & ((row % c_len) >= (col % c_len))
    eye = row == col
    eye_f = eye.astype(F32)
    first_head = (lax.broadcasted_iota(jnp.int32, (c_len, RW_DIM), 1) & RW_HEAD) == 0
    tril = tril_ref[...]

    def halves(x):
        return jnp.where(first_head, x, 0.0).astype(BF16), jnp.where(first_head, 0.0, x).astype(BF16)

    def bd(hv, p):
        sl = slice(p * PAIR, (p + 1) * PAIR)
        return jnp.concatenate([hv[0][:, sl], hv[1][:, sl]], axis=0)

    def chunk_inputs(rows):
        ld = ld_ref[rows, :]
        l1, l2, l3 = _split3(ld)
        cum = _dot(tril, l1) + (_dot(tril, l2) + _dot(tril, l3))
        end = cum[c_len - 1:c_len, :]
        g_inv = jnp.exp(-cum)
        g_end = jnp.exp(end - cum)
        a = a_ref[rows, :].astype(F32)
        b = b_ref[rows, :].astype(F32)
        k = k_ref[rows, :].astype(F32)
        return dict(
            at=halves(a * jnp.exp(cum - ld)), rt=halves(r_ref[rows, :].astype(F32) * jnp.exp(cum)),
            bt=halves(b * g_inv), kt=halves(k * g_inv), bh=halves(b * g_end), kh=halves(k * g_end),
            vv=halves(v_ref[rows, :].astype(F32)), gam_end=jnp.exp(end))

    def iter_body(i, carry):
        base = i * (CHUNKS_PER_ITER * c_len)
        rows = [pl.ds(pl.multiple_of(base + j * c_len, c_len), c_len) for j in range(CHUNKS_PER_ITER)]
        ins = [chunk_inputs(rw) for rw in rows]
        probs = [(j, p) for j in range(CHUNKS_PER_ITER) for p in range(N_PAIRS)]
        form = lambda name: [bd(ins[j][name], p) for j, p in probs]
        bd_at, bd_rt, bd_v = form("at"), form("rt"), form("vv")

        g1 = [_nt(jnp.concatenate([ab, rb], axis=0), jnp.concatenate([bb, kb], axis=0))
              for ab, rb, bb, kb in zip(bd_at, bd_rt, form("bt"), form("kt"))]
        a_ab = [jnp.where(strict, g[:PAIR, :PAIR], 0.0) for g in g1]
        a_ak = [jnp.where(strict, g[:PAIR, PAIR:], 0.0).astype(BF16) for g in g1]
        a_rb = [jnp.where(incl, g[PAIR:, :PAIR], 0.0).astype(BF16) for g in g1]
        a_rk = [jnp.where(incl, g[PAIR:, PAIR:], 0.0).astype(BF16) for g in g1]

        x = [eye_f + m for m in a_ab]
        pw = [m.astype(BF16) for m in a_ab]
        pw = [_dot(m, m).astype(BF16) for m in pw]
        for _ in range(c_len.bit_length() - 3):
            res = [_dot(jnp.concatenate([xm.astype(BF16), pm], axis=0), pm) for xm, pm in zip(x, pw)]
            x = [xm + rs[:PAIR] for xm, rs in zip(x, res)]
            pw = [rs[PAIR:].astype(BF16) for rs in res]
        t_inv = [(xm + _dot(xm.astype(BF16), pm)).astype(BF16) for xm, pm in zip(x, pw)]

        av = [_dot(m, vb).astype(BF16) for m, vb in zip(a_ak, bd_v)]
        w12 = [_dot(t, jnp.concatenate([ab, avp], axis=1)).astype(BF16) for t, ab, avp in zip(t_inv, bd_at, av)]
        qy = [_dot(m, w) for m, w in zip(a_rb, w12)]
        q_hat = [(rb.astype(F32) + q[:, :PAIR]).astype(BF16) for rb, q in zip(bd_rt, qy)]
        y0 = [q[:, PAIR:] + _dot(m, vb) for q, m, vb in zip(qy, a_rk, bd_v)]
        mn = [_tn(w, bb) for w, bb in zip(w12, form("bh"))]
        mt = [(g[:PAIR] + jnp.where(eye, ins[j]["gam_end"][:, p * PAIR:(p + 1) * PAIR], 0.0)).astype(BF16)
              for (j, p), g in zip(probs, mn)]
        nt = [g[PAIR:] + _tn(vb, kb) for g, vb, kb in zip(mn, bd_v, form("kh"))]

        for (j, p), qh_p, y0_p, mt_p, nt_p in zip(probs, q_hat, y0, mt, nt):
            s = state_ref[p]
            sb = s.astype(BF16)
            y_bd = _nt(qh_p, sb) + y0_p
            y_ref[rows[j], p * PAIR:(p + 1) * PAIR] = (y_bd[:c_len] + y_bd[c_len:]).astype(y_ref.dtype)
            state_ref[p] = _dot(sb, mt_p) + nt_p
        return carry

    lax.fori_loop(0, n_iters, iter_body, 0)


def _rwkv_scan(r, ld, k, v, a, b, batch, seq, tc=512):
    steps = seq // tc
    n = r.shape[0]
    blk = pl.BlockSpec((tc, RW_DIM), lambda bb, t: (bb * steps + t, 0))
    ii = jnp.arange(CHUNK)
    tril = (ii[:, None] >= ii[None, :]).astype(BF16)
    return pl.pallas_call(
        functools.partial(_rwkv_scan_kernel, tc // (CHUNK * CHUNKS_PER_ITER)),
        grid=(batch, steps),
        in_specs=[blk] * 6 + [_const_spec((CHUNK, CHUNK))],
        out_specs=blk,
        out_shape=jax.ShapeDtypeStruct((n, RW_DIM), BF16),
        scratch_shapes=[pltpu.VMEM((N_PAIRS, PAIR, PAIR), F32)],
        compiler_params=_cparams(("parallel", "arbitrary")),
        name="rwkv_scan",
    )(r, ld, k, v, a, b, tril)


def _merge_kernel(x_ref, y_ref, g_ref, bonus_ref, yatt_ref, gates_ref, gnw_ref, gnb_ref, ones_ref,
                  worw_ref, wout_ref, npost_ref, o_ref):
    ones_bd = ones_ref[...]
    y = y_ref[...].astype(F32)
    mean = _head_sum(y, ones_bd) * (1.0 / RW_HEAD)
    d = y - mean
    var = _head_sum(d * d, ones_bd) * (1.0 / RW_HEAD)
    yn = d * lax.rsqrt(var + GN_EPS) * gnw_ref[...] + gnb_ref[...]
    rw_out = ((yn + bonus_ref[...].astype(F32)) * g_ref[...].astype(F32)).astype(BF16)
    y_rw = _dot(rw_out, worw_ref[...])
    gates = gates_ref[...].astype(F32)
    mixed = gates[:, :D_MODEL] * yatt_ref[...].astype(F32) + gates[:, D_MODEL:] * y_rw
    out = _dot(mixed.astype(BF16), wout_ref[...])
    o_ref[...] = x_ref[...] + _rmsnorm(out, npost_ref[...])


def _merge(x, y, g, bonus, yatt, gates, gn_w, gn_b, ones_bd, w_o_rw, w_out, norm_post, tm=512):
    n = x.shape[0]
    row = lambda w: pl.BlockSpec((tm, w), lambda i: (i, 0))
    vec = _const_spec((1, D_MODEL))
    mat = _const_spec((D_MODEL, D_MODEL))
    return pl.pallas_call(
        _merge_kernel,
        grid=(n // tm,),
        in_specs=[row(D_MODEL)] * 5 + [row(2 * D_MODEL), vec, vec, _const_spec(ones_bd.shape), mat, mat, vec],
        out_specs=row(D_MODEL),
        out_shape=jax.ShapeDtypeStruct((n, D_MODEL), F32),
        compiler_params=_cparams(("parallel",)),
        name="mixer_merge",
    )(x, y, g, bonus, yatt, gates, gn_w, gn_b, ones_bd, w_o_rw, w_out, norm_post)


def _gelu_tanh(x):
    return 0.5 * x * (1.0 + jnp.tanh(0.7978845608028654 * (x + 0.044715 * (x * x * x))))


def _ffn_kernel(tiles_per_seq, tm, x_ref, p_ref, npre_ref, wup_ref, cw_ref, cb_ref, wdown_ref, npost_ref,
                wple_ref, wpg_ref, nple_ref, o_ref, ubuf_ref):
    first = (pl.program_id(0) % tiles_per_seq) == 0

    @pl.when(first)
    def _():
        ubuf_ref[0:8, :] = jnp.zeros((8, 2 * FFN_DIM), F32)

    @pl.when(jnp.logical_not(first))
    def _():
        ubuf_ref[0:8, :] = ubuf_ref[tm:tm + 8, :]

    x = x_ref[...]
    h = _rmsnorm(x, npre_ref[...]).astype(BF16)
    ubuf_ref[8:tm + 8, :] = _dot(h, wup_ref[...])
    cw = cw_ref[...]
    u = (ubuf_ref[6:tm + 6, :] * cw[0:1, :] + ubuf_ref[7:tm + 7, :] * cw[1:2, :]
         + ubuf_ref[8:tm + 8, :] * cw[2:3, :] + cb_ref[...])
    act = (_gelu_tanh(u[:, :FFN_DIM]) * u[:, FFN_DIM:]).astype(BF16)
    x = x + _rmsnorm(_dot(act, wdown_ref[...]), npost_ref[...])
    e = _dot(p_ref[...].astype(BF16), wple_ref[...])
    gate = _sigmoid(_dot(x.astype(BF16), wpg_ref[...]))
    o_ref[...] = x + _rmsnorm(gate * e, nple_ref[...])


def _ffn_ple(x, p, norm_pre, w_up, conv_w8, conv_b, w_down, norm_post, w_ple, w_pg, norm_ple, seq, tm=256):
    n = x.shape[0]
    row = lambda w: pl.BlockSpec((tm, w), lambda i: (i, 0))
    vec = _const_spec((1, D_MODEL))
    return pl.pallas_call(
        functools.partial(_ffn_kernel, seq // tm, tm),
        grid=(n // tm,),
        in_specs=[row(D_MODEL), row(PLE_DIM), vec, _const_spec(w_up.shape), _const_spec(conv_w8.shape),
                  _const_spec((1, 2 * FFN_DIM)), _const_spec(w_down.shape), vec, _const_spec(w_ple.shape),
                  _const_spec(w_pg.shape), vec],
        out_specs=row(D_MODEL),
        out_shape=jax.ShapeDtypeStruct((n, D_MODEL), F32),
        scratch_shapes=[pltpu.VMEM((tm + 8, 2 * FFN_DIM), F32)],
        compiler_params=_cparams(("arbitrary",)),
        name="ffn_ple",
    )(x, p, norm_pre, w_up, conv_w8, conv_b, w_down, norm_post, w_ple, w_pg, norm_ple)


def _layer(x, p_i, cos, sin, ones_bd, batch, seq, norm_mix_pre, w_in, att_sinks, w_o_att, rw_mu, rw_w0, rw_w2,
           rw_a0, rw_a2, rw_g2, rw_k_k, rw_k_a, rw_r_k, rw_gn_w, rw_gn_b, w_o_rw, w_out, norm_mix_post,
           norm_ffn_pre, w_up, conv_w, conv_b, w_down, norm_ffn_post, w_ple, w_ple_gate, norm_ple):
    vec = lambda t: t.reshape(1, -1)
    o_rw = ATT_Q + 2 * ATT_KV
    o_lora = o_rw + 3 * RW_DIM
    o_gate = o_rw + RW_SHIFT_COLS
    wqkv = w_in[:, :o_rw].astype(BF16)
    wrkv = w_in[:, o_rw:o_lora].astype(BF16)

    def lora_slots(t, axis):
        parts, off = [], 0
        for rank, slot in zip(LORA_RANKS, LORA_SLOTS):
            pad = [(0, 0)] * t.ndim
            pad[axis] = (0, slot - rank)
            parts.append(jnp.pad(lax.slice_in_dim(t, off, off + rank, axis=axis), pad))
            off += rank
        return parts

    wlora = jnp.concatenate(lora_slots(w_in[:, o_lora:o_gate], 1), axis=1).astype(BF16)
    wgate = w_in[:, o_gate:].astype(BF16)
    src = jnp.arange(ATT_KV)[:, None]
    dst = jnp.arange(ATT_Q)[None, :]
    rep = ((src // HEAD_DIM == dst // GROUP_W) & (src % HEAD_DIM == dst % HEAD_DIM)).astype(BF16)
    mu_rkv = vec(rw_mu[:3 * RW_DIM])
    mu_lora = vec(jnp.concatenate(lora_slots(rw_mu[3 * RW_DIM:], 0)))
    wl = jnp.concatenate([jnp.pad(w, ((0, slot - w.shape[0]), (0, 0)))
                          for w, slot in zip((rw_w2, rw_a2, rw_g2), LORA_SLOTS)], axis=0).astype(BF16)
    q, k, v, gates, r, ld, k2, v2, a, b, g, bonus = _mixer_in(
        x, vec(norm_mix_pre), cos, sin, wqkv, rep, wgate, wrkv, wlora, mu_rkv, mu_lora, wl, vec(rw_w0),
        vec(rw_a0), vec(rw_k_k), vec(rw_k_a), vec(rw_r_k), ones_bd, seq)

    sinks_tiled = jnp.broadcast_to(att_sinks.reshape(N_Q_HEADS, 1), (N_Q_HEADS, LANES))
    y_att = _attention(q, k, v, sinks_tiled, w_o_att.astype(BF16), batch, seq)
    y = _rwkv_scan(r, ld, k2, v2, a, b, batch, seq)

    x = _merge(x, y, g, bonus, y_att, gates, vec(rw_gn_w), vec(rw_gn_b), ones_bd, w_o_rw.astype(BF16),
               w_out.astype(BF16), vec(norm_mix_post))

    conv_w8 = jnp.pad(conv_w, ((0, 8 - CONV_W), (0, 0)))
    return _ffn_ple(x, p_i, vec(norm_ffn_pre), w_up.astype(BF16), conv_w8, vec(conv_b), w_down.astype(BF16),
                    vec(norm_ffn_post), w_ple.astype(BF16), w_ple_gate.astype(BF16), vec(norm_ple), seq)


def kernel(x, p, positions, norm_mix_pre, w_in, att_sinks, w_o_att, rw_mu, rw_w0, rw_w2, rw_a0, rw_a2, rw_g2, rw_k_k, rw_k_a, rw_r_k, rw_gn_w, rw_gn_b, w_o_rw, w_out, norm_mix_post, norm_ffn_pre, w_up, conv_w, conv_b, w_down, norm_ffn_post, w_ple, w_ple_gate, norm_ple):
    batch, seq, _ = x.shape
    depth = p.shape[0]
    n = batch * seq
    assert seq % 512 == 0 and x.shape[2] == D_MODEL
    cos, sin = _rope_table(positions.reshape(n, 1).astype(jnp.int32))
    hid = jnp.arange(4 * RW_HEAD) // RW_HEAD
    ones_bd = (hid[:, None] == hid[None, :]).astype(BF16)
    xf = x.reshape(n, D_MODEL)
    per_layer = (norm_mix_pre, w_in, att_sinks, w_o_att, rw_mu, rw_w0, rw_w2, rw_a0, rw_a2, rw_g2, rw_k_k,
                 rw_k_a, rw_r_k, rw_gn_w, rw_gn_b, w_o_rw, w_out, norm_mix_post, norm_ffn_pre, w_up, conv_w,
                 conv_b, w_down, norm_ffn_post, w_ple, w_ple_gate, norm_ple)
    for i in range(depth):
        xf = _layer(xf, p[i].reshape(n, PLE_DIM), cos, sin, ones_bd, batch, seq, *(t[i] for t in per_layer))
    return xf.reshape(batch, seq, D_MODEL)
```

```python
import functools

import jax
import jax.numpy as jnp
from jax import lax
from jax.experimental import pallas as pl
from jax.experimental.pallas import tpu as pltpu

F32 = jnp.float32
BF16 = jnp.bfloat16

D_MODEL = 1024
PLE_DIM = 256
HEAD_DIM = 64
N_Q_HEADS = 16
N_KV_HEADS = 4
Q_PER_KV = N_Q_HEADS // N_KV_HEADS
WINDOW = 128
BLOCK = 128
ROPE_THETA = 10000.0
ATT_Q = N_Q_HEADS * HEAD_DIM
ATT_KV = N_KV_HEADS * HEAD_DIM
RW_HEADS = 16
RW_HEAD = 64
RW_DIM = RW_HEADS * RW_HEAD
DECAY_RANK = 64
ICLR_RANK = 64
GATE_RANK = 160
LORA_DIM = DECAY_RANK + ICLR_RANK + GATE_RANK
LORA_RANKS = (DECAY_RANK, ICLR_RANK, GATE_RANK)
LORA_SLOTS = (128, 128, 256)
LORA_PAD = sum(LORA_SLOTS)
RW_SHIFT_COLS = 3 * RW_DIM + LORA_DIM
FFN_DIM = 2816
CONV_W = 3
NORM_EPS = 1e-6
GN_EPS = 64e-5

LANES = 128
CHUNK = 64
VMEM_LIMIT = 56 * 1024 * 1024


def _cparams(sem):
    return pltpu.CompilerParams(dimension_semantics=sem, vmem_limit_bytes=VMEM_LIMIT)


def _const_spec(shape):
    nd = len(shape)
    return pl.BlockSpec(shape, lambda *_: (0,) * nd, pipeline_mode=pl.Buffered(1))


def _rmsnorm(x, g):
    return x * lax.rsqrt(jnp.mean(x * x, axis=-1, keepdims=True) + NORM_EPS) * g


def _sigmoid(x):
    return 1.0 / (1.0 + jnp.exp(-x))


def _dot(a, b):
    return jnp.dot(a, b, preferred_element_type=F32)


def _split2(x):
    x1 = x.astype(BF16)
    return x1, (x - x1.astype(F32)).astype(BF16)


def _head_sum(x, ones_bd):
    xb = x.astype(BF16)
    w = ones_bd.shape[0]
    return jnp.concatenate([_dot(xb[:, i * w:(i + 1) * w], ones_bd) for i in range(x.shape[1] // w)], axis=1)


def _rope_table_kernel(pos_ref, cos_ref, sin_ref):
    half = HEAD_DIM // 2
    lane = lax.broadcasted_iota(jnp.int32, (1, LANES), 1)
    fidx = (lane & (half - 1)).astype(F32)
    inv_freq = jnp.power(ROPE_THETA, -fidx / half)
    ang = pos_ref[...].astype(F32) * inv_freq
    sign = jnp.where((lane & half) == 0, -1.0, 1.0)
    cos_ref[...] = jnp.cos(ang)
    sin_ref[...] = jnp.sin(ang) * sign


def _rope_table(pos, tm=512):
    n = pos.shape[0]
    return pl.pallas_call(
        _rope_table_kernel,
        grid=(n // tm,),
        in_specs=[pl.BlockSpec((tm, 1), lambda i: (i, 0))],
        out_specs=[pl.BlockSpec((tm, LANES), lambda i: (i, 0))] * 2,
        out_shape=[jax.ShapeDtypeStruct((n, LANES), F32)] * 2,
        compiler_params=_cparams(("parallel",)),
        name="rope_table",
    )(pos)


def _rope(t, cos, sin):
    w = t.shape[1]
    reps = w // LANES
    half = HEAD_DIM // 2
    cos_w = jnp.concatenate([cos] * reps, axis=1)
    sin_w = jnp.concatenate([sin] * reps, axis=1)
    lane = lax.broadcasted_iota(jnp.int32, t.shape, 1)
    swapped = jnp.where((lane & half) == 0, pltpu.roll(t, w - half, 1), pltpu.roll(t, half, 1))
    return t * cos_w + swapped * sin_w


def _shift_rows(z, prev_rows):
    rolled = pltpu.roll(z, 1, 0)
    g = prev_rows.shape[0]
    row = lax.broadcasted_iota(jnp.int32, (g, z.shape[1]), 0)
    head = jnp.where(row == 0, prev_rows[g - 1:, :], rolled[:g])
    return jnp.concatenate([head, rolled[g:]], axis=0)


def _mixer_in_kernel(tiles_per_seq, x_ref, g_ref, cos_ref, sin_ref, wqkv_ref, rep_ref, wgate_ref, wrkv_ref,
                     wlora_ref, mu_rkv_ref, mu_lora_ref, wl_ref, w0_ref, a0_ref, kk_ref, ka_ref, rk_ref, ones_ref,
                     q_ref, k_ref, v_ref, gates_ref, r_ref, ld_ref, k2_ref, v2_ref, a_ref, b_ref, g2_ref, bonus_ref,
                     zrkv_tail_ref, zlora_tail_ref):
    @pl.when((pl.program_id(0) % tiles_per_seq) == 0)
    def _():
        zrkv_tail_ref[...] = jnp.zeros_like(zrkv_tail_ref)
        zlora_tail_ref[...] = jnp.zeros_like(zlora_tail_ref)

    tm = x_ref.shape[0]
    h = _rmsnorm(x_ref[...], g_ref[...]).astype(BF16)
    cos = cos_ref[...]
    sin = sin_ref[...]

    z = _dot(h, wrkv_ref[...])
    zl = _dot(h, wlora_ref[...])
    zs = z + (_shift_rows(z, zrkv_tail_ref[...]) - z) * mu_rkv_ref[...]
    zls = zl + (_shift_rows(zl, zlora_tail_ref[...]) - zl) * mu_lora_ref[...]
    zrkv_tail_ref[...] = z[tm - 8:, :]
    zlora_tail_ref[...] = zl[tm - 8:, :]

    s0, s1 = LORA_SLOTS[0], LORA_SLOTS[0] + LORA_SLOTS[1]
    lo_w = _dot(jnp.tanh(zls[:, :s0]).astype(BF16), wl_ref[:s0, :])
    lo_a = _dot(zls[:, s0:s1].astype(BF16), wl_ref[s0:s1, :])
    lo_g = _dot(_sigmoid(zls[:, s1:]).astype(BF16), wl_ref[s1:, :])
    r = zs[:, :RW_DIM]
    k = zs[:, RW_DIM:2 * RW_DIM]
    v = zs[:, 2 * RW_DIM:]
    wpre = -(w0_ref[...] + lo_w)
    softplus = jnp.maximum(wpre, 0.0) + jnp.log(1.0 + jnp.exp(-jnp.abs(wpre)))
    w = -softplus - 0.5
    iclr = _sigmoid(a0_ref[...] + lo_a)
    ones_bd = ones_ref[...]
    kkr = k * kk_ref[...]
    kk = kkr * lax.rsqrt(jnp.maximum(_head_sum(kkr * kkr, ones_bd), 1e-24))
    k2 = k * (1.0 + (iclr - 1.0) * ka_ref[...])
    r_ref[...] = r.astype(BF16)
    ld_ref[...] = -jnp.exp(w)
    k2_ref[...] = k2.astype(BF16)
    v2_ref[...] = v.astype(BF16)
    a_ref[...] = (-kk).astype(BF16)
    b_ref[...] = (kk * iclr).astype(BF16)
    g2_ref[...] = lo_g.astype(BF16)
    bonus_ref[...] = (_head_sum(r * k2 * rk_ref[...], ones_bd) * v).astype(BF16)

    gates_ref[...] = _sigmoid(_dot(h, wgate_ref[...])).astype(BF16)
    qkv = _dot(h, wqkv_ref[...])
    q_ref[...] = (_rope(qkv[:, :ATT_Q], cos, sin) * (HEAD_DIM ** -0.5)).astype(BF16)
    rep = rep_ref[...]
    k_ref[...] = _dot(_rope(qkv[:, ATT_Q:ATT_Q + ATT_KV], cos, sin).astype(BF16), rep).astype(BF16)
    v_ref[...] = _dot(qkv[:, ATT_Q + ATT_KV:].astype(BF16), rep).astype(BF16)


def _mixer_in(x, g, cos, sin, wqkv, rep, wgate, wrkv, wlora, mu_rkv, mu_lora, wl, w0, a0, k_k, k_a, r_k, ones_bd,
              seq, tm=256):
    n = x.shape[0]
    row = lambda w: pl.BlockSpec((tm, w), lambda i: (i, 0))
    vec = lambda w: _const_spec((1, w))
    consts = [wqkv, rep, wgate, wrkv, wlora]
    out_dtypes = [BF16] * 5 + [F32] + [BF16] * 6
    out_widths = [ATT_Q, ATT_Q, ATT_Q, 2 * D_MODEL] + [RW_DIM] * 8
    return pl.pallas_call(
        functools.partial(_mixer_in_kernel, seq // tm),
        grid=(n // tm,),
        in_specs=[row(D_MODEL), vec(D_MODEL), row(LANES), row(LANES)] + [_const_spec(c.shape) for c in consts]
                 + [vec(3 * RW_DIM), vec(LORA_PAD), _const_spec(wl.shape)] + [vec(RW_DIM)] * 5
                 + [_const_spec(ones_bd.shape)],
        out_specs=[row(w) for w in out_widths],
        out_shape=[jax.ShapeDtypeStruct((n, w), dt) for w, dt in zip(out_widths, out_dtypes)],
        scratch_shapes=[pltpu.VMEM((8, 3 * RW_DIM), F32), pltpu.VMEM((8, LORA_PAD), F32)],
        compiler_params=_cparams(("arbitrary",)),
        name="mixer_in",
    )(x, g, cos, sin, wqkv, rep, wgate, wrkv, wlora, mu_rkv, mu_lora, wl, w0, a0, k_k, k_a, r_k, ones_bd)


GROUP_W = Q_PER_KV * HEAD_DIM
KEYS = 2 * BLOCK
Q_BLOCKS = 8


def _attn_kernel(q_ref, kc_ref, kp_ref, vc_ref, vp_ref, bias_ref, qmask_ref, sink_ref, wo_ref, o_ref):
    q = q_ref[...]
    kall = jnp.concatenate([kp_ref[...], kc_ref[...]], axis=0)
    vall = jnp.concatenate([vp_ref[...], vc_ref[...]], axis=0)
    bias_rest = bias_ref[1]
    bias_first = jnp.where(pl.program_id(1) == 0, bias_ref[0], bias_rest)
    qmask = qmask_ref[...]
    member_of_lane = lax.broadcasted_iota(jnp.int32, (BLOCK, GROUP_W), 1) // HEAD_DIM
    groups = [slice(j * GROUP_W, (j + 1) * GROUP_W) for j in range(N_KV_HEADS)]
    probs = [(blk, j) for blk in range(Q_BLOCKS) for j in range(N_KV_HEADS)]
    q_rows = lambda blk: slice(blk * BLOCK, (blk + 1) * BLOCK)
    k_rows = lambda blk: slice(blk * BLOCK, blk * BLOCK + KEYS)

    qm = [jnp.concatenate([q[q_rows(blk), groups[j]]] * Q_PER_KV, axis=0) * qmask for blk, j in probs]
    s = [_nt(qi, kall[k_rows(blk), groups[j]]) + (bias_first if blk == 0 else bias_rest)
         for qi, (blk, j) in zip(qm, probs)]
    sinks = [jnp.concatenate(
        [jnp.broadcast_to(sink_ref[h:h + 1, 0:1], (BLOCK, 1)) for h in range(j * Q_PER_KV, (j + 1) * Q_PER_KV)],
        axis=0) for _, j in probs]
    m = [jnp.maximum(jnp.max(si, axis=-1, keepdims=True), sk) for si, sk in zip(s, sinks)]
    p = [jnp.exp(si - mi).astype(BF16) for si, mi in zip(s, m)]
    ones_cols = jnp.ones((KEYS, LANES), BF16)
    ov = [_dot(pi, jnp.concatenate([vall[k_rows(blk), groups[j]], ones_cols], axis=1))
          for pi, (blk, j) in zip(p, probs)]
    outs = []
    for oi, sk, mi in zip(ov, sinks, m):
        rinv = 1.0 / (oi[:, GROUP_W:] + jnp.exp(sk - mi))
        on = oi[:, :GROUP_W] * jnp.concatenate([rinv] * (GROUP_W // LANES), axis=1)
        acc = on[(Q_PER_KV - 1) * BLOCK:]
        for g in reversed(range(Q_PER_KV - 1)):
            acc = jnp.where(member_of_lane == g, on[g * BLOCK:(g + 1) * BLOCK], acc)
        outs.append(acc)
    o = jnp.concatenate([jnp.concatenate(outs[blk * N_KV_HEADS:(blk + 1) * N_KV_HEADS], axis=1)
                         for blk in range(Q_BLOCKS)], axis=0).astype(BF16)
    o_ref[...] = _dot(o, wo_ref[...]).astype(o_ref.dtype)


def _attention(q, k, v, sinks_tiled, wo, batch, seq):
    nb = seq // BLOCK
    steps = nb // Q_BLOCKS
    cur = lambda b, n: (b * steps + n, 0)
    prev = lambda b, n: (b * nb + jnp.maximum(n * Q_BLOCKS - 1, 0), 0)
    qi = jnp.arange(BLOCK)[:, None]
    kj = jnp.arange(KEYS)[None, :]
    dist = qi + BLOCK - kj
    band = (dist >= 0) & (dist < WINDOW)
    masks = jnp.stack([band & (kj >= BLOCK), band])
    bias = jnp.tile(jnp.where(masks, 0.0, -jnp.inf).astype(F32), (1, Q_PER_KV, 1))
    member_r = jnp.arange(Q_PER_KV * BLOCK)[:, None] // BLOCK
    member_c = jnp.arange(GROUP_W)[None, :] // HEAD_DIM
    qmask = (member_r == member_c).astype(BF16)
    cur_blk = pl.BlockSpec((Q_BLOCKS * BLOCK, ATT_Q), cur)
    prev_blk = pl.BlockSpec((BLOCK, ATT_Q), prev)
    return pl.pallas_call(
        _attn_kernel,
        grid=(batch, steps),
        in_specs=[cur_blk, cur_blk, prev_blk, cur_blk, prev_blk, _const_spec(bias.shape),
                  _const_spec(qmask.shape), _const_spec(sinks_tiled.shape), _const_spec(wo.shape)],
        out_specs=pl.BlockSpec((Q_BLOCKS * BLOCK, D_MODEL), cur),
        out_shape=jax.ShapeDtypeStruct((batch * seq, D_MODEL), BF16),
        compiler_params=_cparams(("parallel", "parallel")),
        name="swa_attention",
    )(q, k, k, v, v, bias, qmask, sinks_tiled, wo)


PAIR = 2 * RW_HEAD
N_PAIRS = RW_DIM // PAIR


def _nt(a, b):
    return lax.dot_general(a, b, (((1,), (1,)), ((), ())), preferred_element_type=F32)


def _tn(a, b):
    return lax.dot_general(a, b, (((0,), (0,)), ((), ())), preferred_element_type=F32)


CHUNKS_PER_ITER = 2


def _rwkv_scan_kernel(n_iters, r_ref, ld_ref, k_ref, v_ref, a_ref, b_ref, tril_ref, y_ref, state_ref):
    @pl.when(pl.program_id(1) == 0)
    def _():
        state_ref[...] = jnp.zeros_like(state_ref)

    c_len = CHUNK
    row = lax.broadcasted_iota(jnp.int32, (PAIR, PAIR), 0)
    col = lax.broadcasted_iota(jnp.int32, (PAIR, PAIR), 1)
    same_head = (row // c_len) == (col // c_len)
    strict = same_head & ((row % c_len) > (col % c_len))
    incl = same_head & ((row % c_len) >= (col % c_len))
    eye = row == col
    eye_f = eye.astype(F32)
    first_head = (lax.broadcasted_iota(jnp.int32, (c_len, RW_DIM), 1) & RW_HEAD) == 0
    tril = tril_ref[...]

    def halves(x):
        return jnp.where(first_head, x, 0.0).astype(BF16), jnp.where(first_head, 0.0, x).astype(BF16)

    def bd(hv, p):
        sl = slice(p * PAIR, (p + 1) * PAIR)
        return jnp.concatenate([hv[0][:, sl], hv[1][:, sl]], axis=0)

    def chunk_inputs(rows):
        ld = ld_ref[rows, :]
        l1, l2 = _split2(ld)
        cum = _dot(tril, l1) + _dot(tril, l2)
        end = cum[c_len - 1:c_len, :]
        g_inv = jnp.exp(-cum)
        g_end = jnp.exp(end - cum)
        a = a_ref[rows, :].astype(F32)
        b = b_ref[rows, :].astype(F32)
        k = k_ref[rows, :].astype(F32)
        return dict(
            at=halves(a * jnp.exp(cum - ld)), rt=halves(r_ref[rows, :].astype(F32) * jnp.exp(cum)),
            bt=halves(b * g_inv), kt=halves(k * g_inv), bh=halves(b * g_end), kh=halves(k * g_end),
            vv=halves(v_ref[rows, :].astype(F32)), gam_end=jnp.exp(end))

    def iter_body(i, carry):
        base = i * (CHUNKS_PER_ITER * c_len)
        rows = [pl.ds(pl.multiple_of(base + j * c_len, c_len), c_len) for j in range(CHUNKS_PER_ITER)]
        ins = [chunk_inputs(rw) for rw in rows]
        probs = [(j, p) for j in range(CHUNKS_PER_ITER) for p in range(N_PAIRS)]
        form = lambda name: [bd(ins[j][name], p) for j, p in probs]
        bd_at, bd_rt, bd_v = form("at"), form("rt"), form("vv")

        g1 = [_nt(jnp.concatenate([ab, rb], axis=0), jnp.concatenate([bb, kb], axis=0))
              for ab, rb, bb, kb in zip(bd_at, bd_rt, form("bt"), form("kt"))]
        a_ab = [jnp.where(strict, g[:PAIR, :PAIR], 0.0) for g in g1]
        a_ak = [jnp.where(strict, g[:PAIR, PAIR:], 0.0).astype(BF16) for g in g1]
        a_rb = [jnp.where(incl, g[PAIR:, :PAIR], 0.0).astype(BF16) for g in g1]
        a_rk = [jnp.where(incl, g[PAIR:, PAIR:], 0.0).astype(BF16) for g in g1]

        x = [eye_f + m for m in a_ab]
        pw = [m.astype(BF16) for m in a_ab]
        pw = [_dot(m, m).astype(BF16) for m in pw]
        for _ in range(c_len.bit_length() - 3):
            res = [_dot(jnp.concatenate([xm.astype(BF16), pm], axis=0), pm) for xm, pm in zip(x, pw)]
            x = [xm + rs[:PAIR] for xm, rs in zip(x, res)]
            pw = [rs[PAIR:].astype(BF16) for rs in res]
        t_inv = [(xm + _dot(xm.astype(BF16), pm)).astype(BF16) for xm, pm in zip(x, pw)]

        av = [_dot(m, vb).astype(BF16) for m, vb in zip(a_ak, bd_v)]
        w12 = [_dot(t, jnp.concatenate([ab, avp], axis=1)).astype(BF16) for t, ab, avp in zip(t_inv, bd_at, av)]
        qy = [_dot(m, w) for m, w in zip(a_rb, w12)]
        q_hat = [(rb.astype(F32) + q[:, :PAIR]).astype(BF16) for rb, q in zip(bd_rt, qy)]
        y0 = [q[:, PAIR:] + _dot(m, vb) for q, m, vb in zip(qy, a_rk, bd_v)]
        mn = [_tn(w, bb) for w, bb in zip(w12, form("bh"))]
        mt = [(g[:PAIR] + jnp.where(eye, ins[j]["gam_end"][:, p * PAIR:(p + 1) * PAIR], 0.0)).astype(BF16)
              for (j, p), g in zip(probs, mn)]
        nt = [g[PAIR:] + _tn(vb, kb) for g, vb, kb in zip(mn, bd_v, form("kh"))]

        for (j, p), qh_p, y0_p, mt_p, nt_p in zip(probs, q_hat, y0, mt, nt):
            s = state_ref[p]
            sb = s.astype(BF16)
            y_bd = _nt(qh_p, sb) + y0_p
            y_ref[rows[j], p * PAIR:(p + 1) * PAIR] = (y_bd[:c_len] + y_bd[c_len:]).astype(y_ref.dtype)
            state_ref[p] = _dot(sb, mt_p) + nt_p
        return carry

    lax.fori_loop(0, n_iters, iter_body, 0)


def _rwkv_scan(r, ld, k, v, a, b, batch, seq, tc=512):
    steps = seq // tc
    n = r.shape[0]
    blk = pl.BlockSpec((tc, RW_DIM), lambda bb, t: (bb * steps + t, 0))
    ii = jnp.arange(CHUNK)
    tril = (ii[:, None] >= ii[None, :]).astype(BF16)
    return pl.pallas_call(
        functools.partial(_rwkv_scan_kernel, tc // (CHUNK * CHUNKS_PER_ITER)),
        grid=(batch, steps),
        in_specs=[blk] * 6 + [_const_spec((CHUNK, CHUNK))],
        out_specs=blk,
        out_shape=jax.ShapeDtypeStruct((n, RW_DIM), BF16),
        scratch_shapes=[pltpu.VMEM((N_PAIRS, PAIR, PAIR), F32)],
        compiler_params=_cparams(("parallel", "arbitrary")),
        name="rwkv_scan",
    )(r, ld, k, v, a, b, tril)


def _merge_kernel(x_ref, y_ref, g_ref, bonus_ref, yatt_ref, gates_ref, gnw_ref, gnb_ref, ones_ref,
                  worw_ref, wout_ref, npost_ref, o_ref):
    ones_bd = ones_ref[...]
    y = y_ref[...].astype(F32)
    mean = _head_sum(y, ones_bd) * (1.0 / RW_HEAD)
    d = y - mean
    var = _head_sum(d * d, ones_bd) * (1.0 / RW_HEAD)
    yn = d * lax.rsqrt(var + GN_EPS) * gnw_ref[...] + gnb_ref[...]
    rw_out = ((yn + bonus_ref[...].astype(F32)) * g_ref[...].astype(F32)).astype(BF16)
    y_rw = _dot(rw_out, worw_ref[...])
    gates = gates_ref[...].astype(F32)
    mixed = gates[:, :D_MODEL] * yatt_ref[...].astype(F32) + gates[:, D_MODEL:] * y_rw
    out = _dot(mixed.astype(BF16), wout_ref[...])
    o_ref[...] = x_ref[...] + _rmsnorm(out, npost_ref[...])


def _merge(x, y, g, bonus, yatt, gates, gn_w, gn_b, ones_bd, w_o_rw, w_out, norm_post, tm=512):
    n = x.shape[0]
    row = lambda w: pl.BlockSpec((tm, w), lambda i: (i, 0))
    vec = _const_spec((1, D_MODEL))
    mat = _const_spec((D_MODEL, D_MODEL))
    return pl.pallas_call(
        _merge_kernel,
        grid=(n // tm,),
        in_specs=[row(D_MODEL)] * 5 + [row(2 * D_MODEL), vec, vec, _const_spec(ones_bd.shape), mat, mat, vec],
        out_specs=row(D_MODEL),
        out_shape=jax.ShapeDtypeStruct((n, D_MODEL), F32),
        compiler_params=_cparams(("parallel",)),
        name="mixer_merge",
    )(x, y, g, bonus, yatt, gates, gn_w, gn_b, ones_bd, w_o_rw, w_out, norm_post)


def _gelu_tanh(x):
    return 0.5 * x * (1.0 + jnp.tanh(0.7978845608028654 * (x + 0.044715 * (x * x * x))))


def _ffn_kernel(tiles_per_seq, tm, x_ref, p_ref, npre_ref, wup_ref, cw_ref, cb_ref, wdown_ref, npost_ref,
                wple_ref, wpg_ref, nple_ref, o_ref, ubuf_ref):
    first = (pl.program_id(0) % tiles_per_seq) == 0

    @pl.when(first)
    def _():
        ubuf_ref[0:8, :] = jnp.zeros((8, 2 * FFN_DIM), F32)

    @pl.when(jnp.logical_not(first))
    def _():
        ubuf_ref[0:8, :] = ubuf_ref[tm:tm + 8, :]

    x = x_ref[...]
    h = _rmsnorm(x, npre_ref[...]).astype(BF16)
    ubuf_ref[8:tm + 8, :] = _dot(h, wup_ref[...])
    cw = cw_ref[...]
    u = (ubuf_ref[6:tm + 6, :] * cw[0:1, :] + ubuf_ref[7:tm + 7, :] * cw[1:2, :]
         + ubuf_ref[8:tm + 8, :] * cw[2:3, :] + cb_ref[...])
    act = (_gelu_tanh(u[:, :FFN_DIM]) * u[:, FFN_DIM:]).astype(BF16)
    x = x + _rmsnorm(_dot(act, wdown_ref[...]), npost_ref[...])
    e = _dot(p_ref[...].astype(BF16), wple_ref[...])
    gate = _sigmoid(_dot(x.astype(BF16), wpg_ref[...]))
    o_ref[...] = x + _rmsnorm(gate * e, nple_ref[...])


def _ffn_ple(x, p, norm_pre, w_up, conv_w8, conv_b, w_down, norm_post, w_ple, w_pg, norm_ple, seq, tm=256):
    n = x.shape[0]
    row = lambda w: pl.BlockSpec((tm, w), lambda i: (i, 0))
    vec = _const_spec((1, D_MODEL))
    return pl.pallas_call(
        functools.partial(_ffn_kernel, seq // tm, tm),
        grid=(n // tm,),
        in_specs=[row(D_MODEL), row(PLE_DIM), vec, _const_spec(w_up.shape), _const_spec(conv_w8.shape),
                  _const_spec((1, 2 * FFN_DIM)), _const_spec(w_down.shape), vec, _const_spec(w_ple.shape),
                  _const_spec(w_pg.shape), vec],
        out_specs=row(D_MODEL),
        out_shape=jax.ShapeDtypeStruct((n, D_MODEL), F32),
        scratch_shapes=[pltpu.VMEM((tm + 8, 2 * FFN_DIM), F32)],
        compiler_params=_cparams(("arbitrary",)),
        name="ffn_ple",
    )(x, p, norm_pre, w_up, conv_w8, conv_b, w_down, norm_post, w_ple, w_pg, norm_ple)


def _layer(x, p_i, cos, sin, ones_bd, batch, seq, norm_mix_pre, w_in, att_sinks, w_o_att, rw_mu, rw_w0, rw_w2,
           rw_a0, rw_a2, rw_g2, rw_k_k, rw_k_a, rw_r_k, rw_gn_w, rw_gn_b, w_o_rw, w_out, norm_mix_post,
           norm_ffn_pre, w_up, conv_w, conv_b, w_down, norm_ffn_post, w_ple, w_ple_gate, norm_ple):
    vec = lambda t: t.reshape(1, -1)
    o_rw = ATT_Q + 2 * ATT_KV
    o_lora = o_rw + 3 * RW_DIM
    o_gate = o_rw + RW_SHIFT_COLS
    wqkv = w_in[:, :o_rw].astype(BF16)
    wrkv = w_in[:, o_rw:o_lora].astype(BF16)

    def lora_slots(t, axis):
        parts, off = [], 0
        for rank, slot in zip(LORA_RANKS, LORA_SLOTS):
            pad = [(0, 0)] * t.ndim
            pad[axis] = (0, slot - rank)
            parts.append(jnp.pad(lax.slice_in_dim(t, off, off + rank, axis=axis), pad))
            off += rank
        return parts

    wlora = jnp.concatenate(lora_slots(w_in[:, o_lora:o_gate], 1), axis=1).astype(BF16)
    wgate = w_in[:, o_gate:].astype(BF16)
    src = jnp.arange(ATT_KV)[:, None]
    dst = jnp.arange(ATT_Q)[None, :]
    rep = ((src // HEAD_DIM == dst // GROUP_W) & (src % HEAD_DIM == dst % HEAD_DIM)).astype(BF16)
    mu_rkv = vec(rw_mu[:3 * RW_DIM])
    mu_lora = vec(jnp.concatenate(lora_slots(rw_mu[3 * RW_DIM:], 0)))
    wl = jnp.concatenate([jnp.pad(w, ((0, slot - w.shape[0]), (0, 0)))
                          for w, slot in zip((rw_w2, rw_a2, rw_g2), LORA_SLOTS)], axis=0).astype(BF16)
    q, k, v, gates, r, ld, k2, v2, a, b, g, bonus = _mixer_in(
        x, vec(norm_mix_pre), cos, sin, wqkv, rep, wgate, wrkv, wlora, mu_rkv, mu_lora, wl, vec(rw_w0),
        vec(rw_a0), vec(rw_k_k), vec(rw_k_a), vec(rw_r_k), ones_bd, seq)

    sinks_tiled = jnp.broadcast_to(att_sinks.reshape(N_Q_HEADS, 1), (N_Q_HEADS, LANES))
    y_att = _attention(q, k, v, sinks_tiled, w_o_att.astype(BF16), batch, seq)
    y = _rwkv_scan(r, ld, k2, v2, a, b, batch, seq)

    x = _merge(x, y, g, bonus, y_att, gates, vec(rw_gn_w), vec(rw_gn_b), ones_bd, w_o_rw.astype(BF16),
               w_out.astype(BF16), vec(norm_mix_post))

    conv_w8 = jnp.pad(conv_w, ((0, 8 - CONV_W), (0, 0)))
    return _ffn_ple(x, p_i, vec(norm_ffn_pre), w_up.astype(BF16), conv_w8, vec(conv_b), w_down.astype(BF16),
                    vec(norm_ffn_post), w_ple.astype(BF16), w_ple_gate.astype(BF16), vec(norm_ple), seq)


def kernel(x, p, positions, norm_mix_pre, w_in, att_sinks, w_o_att, rw_mu, rw_w0, rw_w2, rw_a0, rw_a2, rw_g2, rw_k_k, rw_k_a, rw_r_k, rw_gn_w, rw_gn_b, w_o_rw, w_out, norm_mix_post, norm_ffn_pre, w_up, conv_w, conv_b, w_down, norm_ffn_post, w_ple, w_ple_gate, norm_ple):
    batch, seq, _ = x.shape
    depth = p.shape[0]
    n = batch * seq
    assert seq % 512 == 0 and x.shape[2] == D_MODEL
    cos, sin = _rope_table(positions.reshape(n, 1).astype(jnp.int32))
    hid = jnp.arange(4 * RW_HEAD) // RW_HEAD
    ones_bd = (hid[:, None] == hid[None, :]).astype(BF16)
    xf = x.reshape(n, D_MODEL)
    per_layer = (norm_mix_pre, w_in, att_sinks, w_o_att, rw_mu, rw_w0, rw_w2, rw_a0, rw_a2, rw_g2, rw_k_k,
                 rw_k_a, rw_r_k, rw_gn_w, rw_gn_b, w_o_rw, w_out, norm_mix_post, norm_ffn_pre, w_up, conv_w,
                 conv_b, w_down, norm_ffn_post, w_ple, w_ple_gate, norm_ple)
    for i in range(depth):
        xf = _layer(xf, p[i].reshape(n, PLE_DIM), cos, sin, ones_bd, batch, seq, *(t[i] for t in per_layer))
    return xf.reshape(batch, seq, D_MODEL)
```

```python
import functools

import jax
import jax.numpy as jnp
from jax import lax
from jax.experimental import pallas as pl
from jax.experimental.pallas import tpu as pltpu

F32 = jnp.float32
BF16 = jnp.bfloat16

D_MODEL = 1024
PLE_DIM = 256
HEAD_DIM = 64
N_Q_HEADS = 16
N_KV_HEADS = 4
Q_PER_KV = N_Q_HEADS // N_KV_HEADS
WINDOW = 128
BLOCK = 128
ROPE_THETA = 10000.0
ATT_Q = N_Q_HEADS * HEAD_DIM
ATT_KV = N_KV_HEADS * HEAD_DIM
RW_HEADS = 16
RW_HEAD = 64
RW_DIM = RW_HEADS * RW_HEAD
DECAY_RANK = 64
ICLR_RANK = 64
GATE_RANK = 160
LORA_DIM = DECAY_RANK + ICLR_RANK + GATE_RANK
LORA_RANKS = (DECAY_RANK, ICLR_RANK, GATE_RANK)
LORA_SLOTS = (128, 128, 256)
LORA_PAD = sum(LORA_SLOTS)
RW_SHIFT_COLS = 3 * RW_DIM + LORA_DIM
FFN_DIM = 2816
CONV_W = 3
NORM_EPS = 1e-6
GN_EPS = 64e-5

LANES = 128
CHUNK = 64
VMEM_LIMIT = 56 * 1024 * 1024


def _cparams(sem):
    return pltpu.CompilerParams(dimension_semantics=sem, vmem_limit_bytes=VMEM_LIMIT)


def _const_spec(shape):
    nd = len(shape)
    return pl.BlockSpec(shape, lambda *_: (0,) * nd, pipeline_mode=pl.Buffered(1))


def _rmsnorm(x, g):
    return x * lax.rsqrt(jnp.mean(x * x, axis=-1, keepdims=True) + NORM_EPS) * g


def _sigmoid(x):
    return 1.0 / (1.0 + jnp.exp(-x))


def _dot(a, b):
    return jnp.dot(a, b, preferred_element_type=F32)


def _split2(x):
    x1 = x.astype(BF16)
    return x1, (x - x1.astype(F32)).astype(BF16)


def _head_sum(x, ones_bd):
    xb = x.astype(BF16)
    w = ones_bd.shape[0]
    return jnp.concatenate([_dot(xb[:, i * w:(i + 1) * w], ones_bd) for i in range(x.shape[1] // w)], axis=1)


def _rope_table_kernel(pos_ref, cos_ref, sin_ref):
    half = HEAD_DIM // 2
    lane = lax.broadcasted_iota(jnp.int32, (1, LANES), 1)
    fidx = (lane & (half - 1)).astype(F32)
    inv_freq = jnp.power(ROPE_THETA, -fidx / half)
    ang = pos_ref[...].astype(F32) * inv_freq
    sign = jnp.where((lane & half) == 0, -1.0, 1.0)
    cos_ref[...] = jnp.cos(ang)
    sin_ref[...] = jnp.sin(ang) * sign


def _rope_table(pos, tm=512):
    n = pos.shape[0]
    return pl.pallas_call(
        _rope_table_kernel,
        grid=(n // tm,),
        in_specs=[pl.BlockSpec((tm, 1), lambda i: (i, 0))],
        out_specs=[pl.BlockSpec((tm, LANES), lambda i: (i, 0))] * 2,
        out_shape=[jax.ShapeDtypeStruct((n, LANES), F32)] * 2,
        compiler_params=_cparams(("parallel",)),
        name="rope_table",
    )(pos)


def _rope(t, cos, sin):
    w = t.shape[1]
    reps = w // LANES
    half = HEAD_DIM // 2
    cos_w = jnp.concatenate([cos] * reps, axis=1)
    sin_w = jnp.concatenate([sin] * reps, axis=1)
    lane = lax.broadcasted_iota(jnp.int32, t.shape, 1)
    swapped = jnp.where((lane & half) == 0, pltpu.roll(t, w - half, 1), pltpu.roll(t, half, 1))
    return t * cos_w + swapped * sin_w


def _shift_rows(z, prev_rows):
    rolled = pltpu.roll(z, 1, 0)
    g = prev_rows.shape[0]
    row = lax.broadcasted_iota(jnp.int32, (g, z.shape[1]), 0)
    head = jnp.where(row == 0, prev_rows[g - 1:, :], rolled[:g])
    return jnp.concatenate([head, rolled[g:]], axis=0)


def _mixer_in_kernel(tiles_per_seq, x_ref, g_ref, cos_ref, sin_ref, wqkv_ref, rep_ref, wgate_ref, wrkv_ref,
                     wlora_ref, mu_rkv_ref, mu_lora_ref, wl_ref, w0_ref, a0_ref, kk_ref, ka_ref, rk_ref, ones_ref,
                     q_ref, k_ref, v_ref, gates_ref, r_ref, ld_ref, k2_ref, v2_ref, a_ref, b_ref, g2_ref, bonus_ref,
                     zrkv_tail_ref, zlora_tail_ref):
    @pl.when((pl.program_id(0) % tiles_per_seq) == 0)
    def _():
        zrkv_tail_ref[...] = jnp.zeros_like(zrkv_tail_ref)
        zlora_tail_ref[...] = jnp.zeros_like(zlora_tail_ref)

    tm = x_ref.shape[0]
    h = _rmsnorm(x_ref[...], g_ref[...]).astype(BF16)
    cos = cos_ref[...]
    sin = sin_ref[...]

    z = _dot(h, wrkv_ref[...])
    zl = _dot(h, wlora_ref[...])
    zs = z + (_shift_rows(z, zrkv_tail_ref[...]) - z) * mu_rkv_ref[...]
    zls = zl + (_shift_rows(zl, zlora_tail_ref[...]) - zl) * mu_lora_ref[...]
    zrkv_tail_ref[...] = z[tm - 8:, :]
    zlora_tail_ref[...] = zl[tm - 8:, :]

    s0, s1 = LORA_SLOTS[0], LORA_SLOTS[0] + LORA_SLOTS[1]
    lo_w = _dot(jnp.tanh(zls[:, :s0]).astype(BF16), wl_ref[:s0, :])
    lo_a = _dot(zls[:, s0:s1].astype(BF16), wl_ref[s0:s1, :])
    lo_g = _dot(_sigmoid(zls[:, s1:]).astype(BF16), wl_ref[s1:, :])
    r = zs[:, :RW_DIM]
    k = zs[:, RW_DIM:2 * RW_DIM]
    v = zs[:, 2 * RW_DIM:]
    wpre = -(w0_ref[...] + lo_w)
    softplus = jnp.maximum(wpre, 0.0) + jnp.log(1.0 + jnp.exp(-jnp.abs(wpre)))
    w = -softplus - 0.5
    iclr = _sigmoid(a0_ref[...] + lo_a)
    ones_bd = ones_ref[...]
    kkr = k * kk_ref[...]
    kk = kkr * lax.rsqrt(jnp.maximum(_head_sum(kkr * kkr, ones_bd), 1e-24))
    k2 = k * (1.0 + (iclr - 1.0) * ka_ref[...])
    r_ref[...] = r.astype(BF16)
    ld_ref[...] = -jnp.exp(w)
    k2_ref[...] = k2.astype(BF16)
    v2_ref[...] = v.astype(BF16)
    a_ref[...] = (-kk).astype(BF16)
    b_ref[...] = (kk * iclr).astype(BF16)
    g2_ref[...] = lo_g.astype(BF16)
    bonus_ref[...] = (_head_sum(r * k2 * rk_ref[...], ones_bd) * v).astype(BF16)

    gates_ref[...] = _sigmoid(_dot(h, wgate_ref[...])).astype(BF16)
    qkv = _dot(h, wqkv_ref[...])
    q_ref[...] = (_rope(qkv[:, :ATT_Q], cos, sin) * (HEAD_DIM ** -0.5)).astype(BF16)
    rep = rep_ref[...]
    k_ref[...] = _dot(_rope(qkv[:, ATT_Q:ATT_Q + ATT_KV], cos, sin).astype(BF16), rep).astype(BF16)
    v_ref[...] = _dot(qkv[:, ATT_Q + ATT_KV:].astype(BF16), rep).astype(BF16)


def _mixer_in(x, g, cos, sin, wqkv, rep, wgate, wrkv, wlora, mu_rkv, mu_lora, wl, w0, a0, k_k, k_a, r_k, ones_bd,
              seq, tm=256):
    n = x.shape[0]
    row = lambda w: pl.BlockSpec((tm, w), lambda i: (i, 0))
    vec = lambda w: _const_spec((1, w))
    consts = [wqkv, rep, wgate, wrkv, wlora]
    out_dtypes = [BF16] * 5 + [F32] + [BF16] * 6
    out_widths = [ATT_Q, ATT_Q, ATT_Q, 2 * D_MODEL] + [RW_DIM] * 8
    return pl.pallas_call(
        functools.partial(_mixer_in_kernel, seq // tm),
        grid=(n // tm,),
        in_specs=[row(D_MODEL), vec(D_MODEL), row(LANES), row(LANES)] + [_const_spec(c.shape) for c in consts]
                 + [vec(3 * RW_DIM), vec(LORA_PAD), _const_spec(wl.shape)] + [vec(RW_DIM)] * 5
                 + [_const_spec(ones_bd.shape)],
        out_specs=[row(w) for w in out_widths],
        out_shape=[jax.ShapeDtypeStruct((n, w), dt) for w, dt in zip(out_widths, out_dtypes)],
        scratch_shapes=[pltpu.VMEM((8, 3 * RW_DIM), F32), pltpu.VMEM((8, LORA_PAD), F32)],
        compiler_params=_cparams(("arbitrary",)),
        name="mixer_in",
    )(x, g, cos, sin, wqkv, rep, wgate, wrkv, wlora, mu_rkv, mu_lora, wl, w0, a0, k_k, k_a, r_k, ones_bd)


GROUP_W = Q_PER_KV * HEAD_DIM
KEYS = 2 * BLOCK
Q_BLOCKS = 8


def _attn_kernel(q_ref, kc_ref, kp_ref, vc_ref, vp_ref, bias_ref, qmask_ref, sink_ref, wo_ref, o_ref):
    q = q_ref[...]
    kall = jnp.concatenate([kp_ref[...], kc_ref[...]], axis=0)
    vall = jnp.concatenate([vp_ref[...], vc_ref[...]], axis=0)
    bias_rest = bias_ref[1]
    bias_first = jnp.where(pl.program_id(1) == 0, bias_ref[0], bias_rest)
    qmask = qmask_ref[...]
    member_of_lane = lax.broadcasted_iota(jnp.int32, (BLOCK, GROUP_W), 1) // HEAD_DIM
    groups = [slice(j * GROUP_W, (j + 1) * GROUP_W) for j in range(N_KV_HEADS)]
    probs = [(blk, j) for blk in range(Q_BLOCKS) for j in range(N_KV_HEADS)]
    q_rows = lambda blk: slice(blk * BLOCK, (blk + 1) * BLOCK)
    k_rows = lambda blk: slice(blk * BLOCK, blk * BLOCK + KEYS)

    qm = [jnp.concatenate([q[q_rows(blk), groups[j]]] * Q_PER_KV, axis=0) * qmask for blk, j in probs]
    s = [_nt(qi, kall[k_rows(blk), groups[j]]) + (bias_first if blk == 0 else bias_rest)
         for qi, (blk, j) in zip(qm, probs)]
    sinks = [jnp.concatenate(
        [jnp.broadcast_to(sink_ref[h:h + 1, 0:1], (BLOCK, 1)) for h in range(j * Q_PER_KV, (j + 1) * Q_PER_KV)],
        axis=0) for _, j in probs]
    m = [jnp.maximum(jnp.max(si, axis=-1, keepdims=True), sk) for si, sk in zip(s, sinks)]
    p = [jnp.exp(si - mi).astype(BF16) for si, mi in zip(s, m)]
    ones_cols = jnp.ones((KEYS, LANES), BF16)
    ov = [_dot(pi, jnp.concatenate([vall[k_rows(blk), groups[j]], ones_cols], axis=1))
          for pi, (blk, j) in zip(p, probs)]
    outs = []
    for oi, sk, mi in zip(ov, sinks, m):
        rinv = 1.0 / (oi[:, GROUP_W:] + jnp.exp(sk - mi))
        on = oi[:, :GROUP_W] * jnp.concatenate([rinv] * (GROUP_W // LANES), axis=1)
        acc = on[(Q_PER_KV - 1) * BLOCK:]
        for g in reversed(range(Q_PER_KV - 1)):
            acc = jnp.where(member_of_lane == g, on[g * BLOCK:(g + 1) * BLOCK], acc)
        outs.append(acc)
    o = jnp.concatenate([jnp.concatenate(outs[blk * N_KV_HEADS:(blk + 1) * N_KV_HEADS], axis=1)
                         for blk in range(Q_BLOCKS)], axis=0).astype(BF16)
    o_ref[...] = _dot(o, wo_ref[...]).astype(o_ref.dtype)


def _attention(q, k, v, sinks_tiled, wo, batch, seq):
    nb = seq // BLOCK
    steps = nb // Q_BLOCKS
    cur = lambda b, n: (b * steps + n, 0)
    prev = lambda b, n: (b * nb + jnp.maximum(n * Q_BLOCKS - 1, 0), 0)
    qi = jnp.arange(BLOCK)[:, None]
    kj = jnp.arange(KEYS)[None, :]
    dist = qi + BLOCK - kj
    band = (dist >= 0) & (dist < WINDOW)
    masks = jnp.stack([band & (kj >= BLOCK), band])
    bias = jnp.tile(jnp.where(masks, 0.0, -jnp.inf).astype(F32), (1, Q_PER_KV, 1))
    member_r = jnp.arange(Q_PER_KV * BLOCK)[:, None] // BLOCK
    member_c = jnp.arange(GROUP_W)[None, :] // HEAD_DIM
    qmask = (member_r == member_c).astype(BF16)
    cur_blk = pl.BlockSpec((Q_BLOCKS * BLOCK, ATT_Q), cur)
    prev_blk = pl.BlockSpec((BLOCK, ATT_Q), prev)
    return pl.pallas_call(
        _attn_kernel,
        grid=(batch, steps),
        in_specs=[cur_blk, cur_blk, prev_blk, cur_blk, prev_blk, _const_spec(bias.shape),
                  _const_spec(qmask.shape), _const_spec(sinks_tiled.shape), _const_spec(wo.shape)],
        out_specs=pl.BlockSpec((Q_BLOCKS * BLOCK, D_MODEL), cur),
        out_shape=jax.ShapeDtypeStruct((batch * seq, D_MODEL), BF16),
        compiler_params=_cparams(("parallel", "parallel")),
        name="swa_attention",
    )(q, k, k, v, v, bias, qmask, sinks_tiled, wo)


PAIR = 2 * RW_HEAD
N_PAIRS = RW_DIM // PAIR


def _nt(a, b):
    return lax.dot_general(a, b, (((1,), (1,)), ((), ())), preferred_element_type=F32)


def _tn(a, b):
    return lax.dot_general(a, b, (((0,), (0,)), ((), ())), preferred_element_type=F32)


CHUNKS_PER_ITER = 2


def _rwkv_scan_kernel(n_iters, r_ref, ld_ref, k_ref, v_ref, a_ref, b_ref, tril_ref, y_ref, state_ref):
    @pl.when(pl.program_id(1) == 0)
    def _():
        state_ref[...] = jnp.zeros_like(state_ref)

    c_len = CHUNK
    row = lax.broadcasted_iota(jnp.int32, (PAIR, PAIR), 0)
    col = lax.broadcasted_iota(jnp.int32, (PAIR, PAIR), 1)
    same_head = (row // c_len) == (col // c_len)
    strict = same_head & ((row % c_len) > (col % c_len))
    incl = same_head & ((row % c_len) >= (col % c_len))
    eye = row == col
    eye_f = eye.astype(F32)
    first_head = (lax.broadcasted_iota(jnp.int32, (c_len, RW_DIM), 1) & RW_HEAD) == 0
    tril = tril_ref[...]

    def halves(x):
        return jnp.where(first_head, x, 0.0).astype(BF16), jnp.where(first_head, 0.0, x).astype(BF16)

    def bd(hv, p):
        sl = slice(p * PAIR, (p + 1) * PAIR)
        return jnp.concatenate([hv[0][:, sl], hv[1][:, sl]], axis=0)

    def chunk_inputs(rows):
        ld = ld_ref[rows, :]
        l1, l2 = _split2(ld)
        cum = _dot(tril, l1) + _dot(tril, l2)
        end = cum[c_len - 1:c_len, :]
        g_inv = jnp.exp(-cum)
        g_end = jnp.exp(end - cum)
        a = a_ref[rows, :].astype(F32)
        b = b_ref[rows, :].astype(F32)
        k = k_ref[rows, :].astype(F32)
        return dict(
            at=halves(a * jnp.exp(cum - ld)), rt=halves(r_ref[rows, :].astype(F32) * jnp.exp(cum)),
            bt=halves(b * g_inv), kt=halves(k * g_inv), bh=halves(b * g_end), kh=halves(k * g_end),
            vv=halves(v_ref[rows, :].astype(F32)), gam_end=jnp.exp(end))

    def iter_body(i, carry):
        base = i * (CHUNKS_PER_ITER * c_len)
        rows = [pl.ds(pl.multiple_of(base + j * c_len, c_len), c_len) for j in range(CHUNKS_PER_ITER)]
        ins = [chunk_inputs(rw) for rw in rows]
        probs = [(j, p) for j in range(CHUNKS_PER_ITER) for p in range(N_PAIRS)]
        form = lambda name: [bd(ins[j][name], p) for j, p in probs]
        bd_at, bd_rt, bd_v = form("at"), form("rt"), form("vv")

        g1 = [_nt(jnp.concatenate([ab, rb], axis=0), jnp.concatenate([bb, kb], axis=0))
              for ab, rb, bb, kb in zip(bd_at, bd_rt, form("bt"), form("kt"))]
        a_ab = [jnp.where(strict, g[:PAIR, :PAIR], 0.0) for g in g1]
        a_ak = [jnp.where(strict, g[:PAIR, PAIR:], 0.0).astype(BF16) for g in g1]
        a_rb = [jnp.where(incl, g[PAIR:, :PAIR], 0.0).astype(BF16) for g in g1]
        a_rk = [jnp.where(incl, g[PAIR:, PAIR:], 0.0).astype(BF16) for g in g1]

        x = [eye_f + m for m in a_ab]
        pw = [m.astype(BF16) for m in a_ab]
        pw = [_dot(m, m).astype(BF16) for m in pw]
        for _ in range(c_len.bit_length() - 3):
            res = [_dot(jnp.concatenate([xm.astype(BF16), pm], axis=0), pm) for xm, pm in zip(x, pw)]
            x = [xm + rs[:PAIR] for xm, rs in zip(x, res)]
            pw = [rs[PAIR:].astype(BF16) for rs in res]
        t_inv = [(xm + _dot(xm.astype(BF16), pm)).astype(BF16) for xm, pm in zip(x, pw)]

        av = [_dot(m, vb).astype(BF16) for m, vb in zip(a_ak, bd_v)]
        w12 = [_dot(t, jnp.concatenate([ab, avp], axis=1)).astype(BF16) for t, ab, avp in zip(t_inv, bd_at, av)]
        qy = [_dot(m, w) for m, w in zip(a_rb, w12)]
        q_hat = [(rb.astype(F32) + q[:, :PAIR]).astype(BF16) for rb, q in zip(bd_rt, qy)]
        y0 = [q[:, PAIR:] + _dot(m, vb) for q, m, vb in zip(qy, a_rk, bd_v)]
        mn = [_tn(w, bb) for w, bb in zip(w12, form("bh"))]
        mt = [(g[:PAIR] + jnp.where(eye, ins[j]["gam_end"][:, p * PAIR:(p + 1) * PAIR], 0.0)).astype(BF16)
              for (j, p), g in zip(probs, mn)]
        nt = [g[PAIR:] + _tn(vb, kb) for g, vb, kb in zip(mn, bd_v, form("kh"))]

        for (j, p), qh_p, y0_p, mt_p, nt_p in zip(probs, q_hat, y0, mt, nt):
            s = state_ref[p]
            sb = s.astype(BF16)
            y_bd = _nt(qh_p, sb) + y0_p
            y_ref[rows[j], p * PAIR:(p + 1) * PAIR] = (y_bd[:c_len] + y_bd[c_len:]).astype(y_ref.dtype)
            state_ref[p] = _dot(sb, mt_p) + nt_p
        return carry

    lax.fori_loop(0, n_iters, iter_body, 0)


def _rwkv_scan(r, ld, k, v, a, b, batch, seq, tc=512):
    steps = seq // tc
    n = r.shape[0]
    blk = pl.BlockSpec((tc, RW_DIM), lambda bb, t: (bb * steps + t, 0))
    ii = jnp.arange(CHUNK)
    tril = (ii[:, None] >= ii[None, :]).astype(BF16)
    return pl.pallas_call(
        functools.partial(_rwkv_scan_kernel, tc // (CHUNK * CHUNKS_PER_ITER)),
        grid=(batch, steps),
        in_specs=[blk] * 6 + [_const_spec((CHUNK, CHUNK))],
        out_specs=blk,
        out_shape=jax.ShapeDtypeStruct((n, RW_DIM), BF16),
        scratch_shapes=[pltpu.VMEM((N_PAIRS, PAIR, PAIR), F32)],
        compiler_params=_cparams(("parallel", "arbitrary")),
        name="rwkv_scan",
    )(r, ld, k, v, a, b, tril)


def _merge_kernel(x_ref, y_ref, g_ref, bonus_ref, yatt_ref, gates_ref, gnw_ref, gnb_ref, ones_ref,
                  worw_ref, wout_ref, npost_ref, o_ref):
    ones_bd = ones_ref[...]
    y = y_ref[...].astype(F32)
    mean = _head_sum(y, ones_bd) * (1.0 / RW_HEAD)
    d = y - mean
    var = _head_sum(d * d, ones_bd) * (1.0 / RW_HEAD)
    yn = d * lax.rsqrt(var + GN_EPS) * gnw_ref[...] + gnb_ref[...]
    rw_out = ((yn + bonus_ref[...].astype(F32)) * g_ref[...].astype(F32)).astype(BF16)
    y_rw = _dot(rw_out, worw_ref[...])
    gates = gates_ref[...].astype(F32)
    mixed = gates[:, :D_MODEL] * yatt_ref[...].astype(F32) + gates[:, D_MODEL:] * y_rw
    out = _dot(mixed.astype(BF16), wout_ref[...])
    o_ref[...] = x_ref[...] + _rmsnorm(out, npost_ref[...])


def _merge(x, y, g, bonus, yatt, gates, gn_w, gn_b, ones_bd, w_o_rw, w_out, norm_post, tm=512):
    n = x.shape[0]
    row = lambda w: pl.BlockSpec((tm, w), lambda i: (i, 0))
    vec = _const_spec((1, D_MODEL))
    mat = _const_spec((D_MODEL, D_MODEL))
    return pl.pallas_call(
        _merge_kernel,
        grid=(n // tm,),
        in_specs=[row(D_MODEL)] * 5 + [row(2 * D_MODEL), vec, vec, _const_spec(ones_bd.shape), mat, mat, vec],
        out_specs=row(D_MODEL),
        out_shape=jax.ShapeDtypeStruct((n, D_MODEL), F32),
        compiler_params=_cparams(("parallel",)),
        name="mixer_merge",
    )(x, y, g, bonus, yatt, gates, gn_w, gn_b, ones_bd, w_o_rw, w_out, norm_post)


FFN_TAIL_ROWS = 128


def _gelu_tanh(x):
    return 0.5 * x * (1.0 + jnp.tanh(0.7978845608028654 * (x + 0.044715 * (x * x * x))))


def _ffn_kernel(tiles_per_seq, tm, x_ref, p_ref, npre_ref, wup_ref, cw_ref, cb_ref, wdown_ref, npost_ref,
                wple_ref, wpg_ref, nple_ref, o_ref, ubuf_ref):
    first = (pl.program_id(0) % tiles_per_seq) == 0

    @pl.when(first)
    def _():
        ubuf_ref[0:8, :] = jnp.zeros((8, 2 * FFN_DIM), F32)

    @pl.when(jnp.logical_not(first))
    def _():
        ubuf_ref[0:8, :] = ubuf_ref[tm:tm + 8, :]

    x = x_ref[...]
    h = _rmsnorm(x, npre_ref[...]).astype(BF16)
    ubuf_ref[8:tm + 8, :] = _dot(h, wup_ref[...])
    cw = cw_ref[...]
    u = (ubuf_ref[6:tm + 6, :] * cw[0:1, :] + ubuf_ref[7:tm + 7, :] * cw[1:2, :]
         + ubuf_ref[8:tm + 8, :] * cw[2:3, :] + cb_ref[...])
    act = (_gelu_tanh(u[:, :FFN_DIM]) * u[:, FFN_DIM:]).astype(BF16)
    groups = [slice(r, r + FFN_TAIL_ROWS) for r in range(0, tm, FFN_TAIL_ROWS)]
    down = [_dot(act[g], wdown_ref[...]) for g in groups]
    e = [_dot(p_ref[g, :].astype(BF16), wple_ref[...]) for g in groups]
    x1 = [x[g] + _rmsnorm(d, npost_ref[...]) for g, d in zip(groups, down)]
    gate = [_sigmoid(_dot(xg.astype(BF16), wpg_ref[...])) for xg in x1]
    for g, xg, gt, eg in zip(groups, x1, gate, e):
        o_ref[g, :] = xg + _rmsnorm(gt * eg, nple_ref[...])


def _ffn_ple(x, p_all, layer, norm_pre, w_up, conv_w8, conv_b, w_down, norm_post, w_ple, w_pg, norm_ple, seq, tm=512):
    n = x.shape[0]
    row = lambda w: pl.BlockSpec((tm, w), lambda i: (i, 0))
    p_row = pl.BlockSpec((tm, PLE_DIM), lambda i: (layer * (n // tm) + i, 0))
    vec = _const_spec((1, D_MODEL))
    p = p_all
    return pl.pallas_call(
        functools.partial(_ffn_kernel, seq // tm, tm),
        grid=(n // tm,),
        in_specs=[row(D_MODEL), p_row, vec, _const_spec(w_up.shape), _const_spec(conv_w8.shape),
                  _const_spec((1, 2 * FFN_DIM)), _const_spec(w_down.shape), vec, _const_spec(w_ple.shape),
                  _const_spec(w_pg.shape), vec],
        out_specs=row(D_MODEL),
        out_shape=jax.ShapeDtypeStruct((n, D_MODEL), F32),
        scratch_shapes=[pltpu.VMEM((tm + 8, 2 * FFN_DIM), F32)],
        compiler_params=_cparams(("arbitrary",)),
        name="ffn_ple",
    )(x, p, norm_pre, w_up, conv_w8, conv_b, w_down, norm_post, w_ple, w_pg, norm_ple)


def _layer(x, p_all, layer, cos, sin, ones_bd, batch, seq, norm_mix_pre, w_in, att_sinks, w_o_att, rw_mu, rw_w0, rw_w2,
           rw_a0, rw_a2, rw_g2, rw_k_k, rw_k_a, rw_r_k, rw_gn_w, rw_gn_b, w_o_rw, w_out, norm_mix_post,
           norm_ffn_pre, w_up, conv_w, conv_b, w_down, norm_ffn_post, w_ple, w_ple_gate, norm_ple):
    vec = lambda t: t.reshape(1, -1)
    o_rw = ATT_Q + 2 * ATT_KV
    o_lora = o_rw + 3 * RW_DIM
    o_gate = o_rw + RW_SHIFT_COLS
    wqkv = w_in[:, :o_rw].astype(BF16)
    wrkv = w_in[:, o_rw:o_lora].astype(BF16)

    def lora_slots(t, axis):
        parts, off = [], 0
        for rank, slot in zip(LORA_RANKS, LORA_SLOTS):
            pad = [(0, 0)] * t.ndim
            pad[axis] = (0, slot - rank)
            parts.append(jnp.pad(lax.slice_in_dim(t, off, off + rank, axis=axis), pad))
            off += rank
        return parts

    wlora = jnp.concatenate(lora_slots(w_in[:, o_lora:o_gate], 1), axis=1).astype(BF16)
    wgate = w_in[:, o_gate:].astype(BF16)
    src = jnp.arange(ATT_KV)[:, None]
    dst = jnp.arange(ATT_Q)[None, :]
    rep = ((src // HEAD_DIM == dst // GROUP_W) & (src % HEAD_DIM == dst % HEAD_DIM)).astype(BF16)
    mu_rkv = vec(rw_mu[:3 * RW_DIM])
    mu_lora = vec(jnp.concatenate(lora_slots(rw_mu[3 * RW_DIM:], 0)))
    wl = jnp.concatenate([jnp.pad(w, ((0, slot - w.shape[0]), (0, 0)))
                          for w, slot in zip((rw_w2, rw_a2, rw_g2), LORA_SLOTS)], axis=0).astype(BF16)
    q, k, v, gates, r, ld, k2, v2, a, b, g, bonus = _mixer_in(
        x, vec(norm_mix_pre), cos, sin, wqkv, rep, wgate, wrkv, wlora, mu_rkv, mu_lora, wl, vec(rw_w0),
        vec(rw_a0), vec(rw_k_k), vec(rw_k_a), vec(rw_r_k), ones_bd, seq)

    sinks_tiled = jnp.broadcast_to(att_sinks.reshape(N_Q_HEADS, 1), (N_Q_HEADS, LANES))
    y_att = _attention(q, k, v, sinks_tiled, w_o_att.astype(BF16), batch, seq)
    y = _rwkv_scan(r, ld, k2, v2, a, b, batch, seq)

    x = _merge(x, y, g, bonus, y_att, gates, vec(rw_gn_w), vec(rw_gn_b), ones_bd, w_o_rw.astype(BF16),
               w_out.astype(BF16), vec(norm_mix_post))

    conv_w8 = jnp.pad(conv_w, ((0, 8 - CONV_W), (0, 0)))
    return _ffn_ple(x, p_all, layer, vec(norm_ffn_pre), w_up.astype(BF16), conv_w8, vec(conv_b), w_down.astype(BF16),
                    vec(norm_ffn_post), w_ple.astype(BF16), w_ple_gate.astype(BF16), vec(norm_ple), seq)


def kernel(x, p, positions, norm_mix_pre, w_in, att_sinks, w_o_att, rw_mu, rw_w0, rw_w2, rw_a0, rw_a2, rw_g2, rw_k_k, rw_k_a, rw_r_k, rw_gn_w, rw_gn_b, w_o_rw, w_out, norm_mix_post, norm_ffn_pre, w_up, conv_w, conv_b, w_down, norm_ffn_post, w_ple, w_ple_gate, norm_ple):
    batch, seq, _ = x.shape
    depth = p.shape[0]
    n = batch * seq
    assert seq % 512 == 0 and x.shape[2] == D_MODEL
    cos, sin = _rope_table(positions.reshape(n, 1).astype(jnp.int32))
    hid = jnp.arange(4 * RW_HEAD) // RW_HEAD
    ones_bd = (hid[:, None] == hid[None, :]).astype(BF16)
    xf = x.reshape(n, D_MODEL)
    per_layer = (norm_mix_pre, w_in, att_sinks, w_o_att, rw_mu, rw_w0, rw_w2, rw_a0, rw_a2, rw_g2, rw_k_k,
                 rw_k_a, rw_r_k, rw_gn_w, rw_gn_b, w_o_rw, w_out, norm_mix_post, norm_ffn_pre, w_up, conv_w,
                 conv_b, w_down, norm_ffn_post, w_ple, w_ple_gate, norm_ple)
    for i in range(depth):
        xf = _layer(xf, p.reshape(depth * n, PLE_DIM), i, cos, sin, ones_bd, batch, seq, *(t[i] for t in per_layer))
    return xf.reshape(batch, seq, D_MODEL)
```
